```python
import jax
import jax.numpy as jnp
from jax import lax
import numpy as np

D_MODEL = 2048
BATCH = 2
SEQ = 4096
DEPTH = 4
DEC_BATCH = 128
DEC_SEQ = 8
PAST_LEN = 8192
PAGE_SIZE = 128

N_EVEN = (DEPTH + 1) // 2
N_ODD = DEPTH // 2
MLA_HEADS = 8
MLA_Q_LORA = 512
MLA_KV_LORA = 256
MLA_NOPE = 128
MLA_ROPE = 64
MLA_V = 128
ROPE_THETA = 10000.0
MLA_SCALE = (MLA_NOPE + MLA_ROPE) ** -0.5
NSA_HEADS = 16
NSA_KV_HEADS = 2
NSA_GROUP = NSA_HEADS // NSA_KV_HEADS
NSA_HEAD_DIM = 64
NSA_SCALE = NSA_HEAD_DIM ** -0.5
CMP_STRIDE = 16
CMP_BLOCK = 2 * CMP_STRIDE
SEL_BLOCK = 64
SEL_TOP = 16
WINDOW = 512
FORCE_BONUS = 1.0e4
DN_QK_HEADS = 16
DN_V_HEADS = 32
DN_K_DIM = 128
DN_V_DIM = 128
DN_CONV = 4
DN_CHUNK = 64
DN_CONV_DIM = 2 * DN_QK_HEADS * DN_K_DIM + DN_V_HEADS * DN_V_DIM
D_FF = 5632
Q_BLOCK = 128
NORM_EPS = 1e-6
L2_EPS = 1e-6
NEG_BIG = -1e30
TINY = 1e-30
MIX_WIDTH = MLA_HEADS * MLA_V + NSA_HEADS * NSA_HEAD_DIM
NSA_KV_W = 2 * NSA_KV_HEADS * NSA_HEAD_DIM
EVEN_IN = MLA_Q_LORA + MLA_KV_LORA + MLA_ROPE + NSA_HEADS * NSA_HEAD_DIM + 3 * NSA_KV_W + 3 * NSA_HEADS
ODD_IN = DN_CONV_DIM + DN_V_HEADS * DN_V_DIM + 2 * DN_V_HEADS

kernel_name = 'hybrid_mla_nsa_gdn_macaron_step'

F32 = jnp.float32


def rmsnorm(x, g):
    xf = x.astype(F32)
    y = xf * lax.rsqrt(jnp.mean(xf * xf, axis=-1, keepdims=True) + NORM_EPS)
    return (y * g.astype(F32)).astype(x.dtype)


def l2norm(x):
    xf = x.astype(F32)
    return xf * lax.rsqrt(jnp.sum(xf * xf, axis=-1, keepdims=True) + L2_EPS)


def masked_softmax(s, mask):
    s = jnp.where(mask, s.astype(F32), NEG_BIG)
    e = jnp.exp(s - jnp.max(s, axis=-1, keepdims=True)) * mask
    return e / jnp.maximum(jnp.sum(e, axis=-1, keepdims=True), TINY)


def rope(x, pos):
    half = x.shape[-1] // 2
    inv = ROPE_THETA ** (-2.0 * jnp.arange(half, dtype=F32) / x.shape[-1])
    ang = pos.astype(F32)[:, None] * inv
    shape = (pos.shape[0],) + (1,) * (x.ndim - 3) + (half,)
    cos = jnp.cos(ang).reshape(shape)
    sin = jnp.sin(ang).reshape(shape)
    x1 = x[..., :half].astype(F32)
    x2 = x[..., half:].astype(F32)
    return jnp.concatenate([x1 * cos - x2 * sin, x1 * sin + x2 * cos], axis=-1).astype(x.dtype)


def alibi_slopes():
    h = jnp.arange(NSA_HEADS, dtype=F32) + 1.0
    return (2.0 ** (-8.0 * h / NSA_HEADS)).reshape(NSA_KV_HEADS, NSA_GROUP)


def pad_rows(x, length):
    return jnp.pad(x, [(0, 0), (0, length - x.shape[1])] + [(0, 0)] * (x.ndim - 2))


def ffn_half(x, g_pre, g_post, w_in, w_out):
    gate, up = jnp.split(rmsnorm(x, g_pre) @ w_in, 2, axis=-1)
    return x + 0.5 * rmsnorm((jax.nn.silu(gate) * up) @ w_out, g_post)


def mla_attend(q_abs, q_rope, segments, q_pos):
    scores, masks = [], []
    for ckv, k_pos in segments:
        s = (jnp.einsum('bthc,bsc->bhts', q_abs, ckv[..., :MLA_KV_LORA])
             + jnp.einsum('bthr,bsr->bhts', q_rope, ckv[..., MLA_KV_LORA:]))
        scores.append(s.astype(F32))
        masks.append(k_pos[None, :] <= q_pos[:, None])
    p = masked_softmax(jnp.concatenate(scores, axis=-1), jnp.concatenate(masks, axis=-1))
    outs, start = [], 0
    for ckv, k_pos in segments:
        n = k_pos.shape[0]
        outs.append(jnp.einsum('bhts,bsc->bthc', p[..., start:start + n].astype(ckv.dtype), ckv[..., :MLA_KV_LORA]))
        start = start + n
    o = outs[0]
    for extra in outs[1:]:
        o = o + extra
    return o


def mla_up(o_lat, w_kv_up):
    B, T = o_lat.shape[:2]
    return jnp.einsum('bthc,chv->bthv', o_lat, w_kv_up[..., MLA_NOPE:]).reshape(B, T, MLA_HEADS * MLA_V)


def chunk_partials(rows, w):
    B, L = rows.shape[:2]
    rows = pad_rows(rows, L + (-L) % CMP_STRIDE)
    ch = rows.reshape((B, rows.shape[1] // CMP_STRIDE, CMP_STRIDE) + rows.shape[2:])
    first = jnp.einsum('bcjegd,ejg->bcegd', ch, w[:, :CMP_STRIDE])
    second = jnp.einsum('bcjegd,ejg->bcegd', ch, w[:, CMP_STRIDE:])
    return first, second


def compressed_blocks(first, second):
    kc = first[:, :-1] + second[:, 1:]
    c_pos = jnp.arange(kc.shape[1]) * CMP_STRIDE + (CMP_BLOCK - 1)
    return kc, c_pos


def gather_blocks(blocks, idx):
    b = jnp.arange(blocks.shape[0])[:, None, None, None]
    g = jnp.arange(NSA_KV_HEADS)[None, None, :, None]
    return blocks[b, idx, :, :, g]


def gather_pool_blocks(pool, page_table, idx):
    bpp = PAGE_SIZE // SEL_BLOCK
    pool_blocks = pool.reshape((pool.shape[0] * bpp, SEL_BLOCK) + pool.shape[2:])
    b = jnp.arange(page_table.shape[0])[:, None, None, None]
    g = jnp.arange(NSA_KV_HEADS)[None, None, :, None]
    phys = page_table[b, idx // bpp] * bpp + idx % bpp
    return pool_blocks[phys, :, :, g]


def nsa_core(qg, gates, q_pos, kc, c_pos, nsb, gather, kw, kw_pos):
    B, Tq = qg.shape[:2]
    slopes = alibi_slopes()
    dc = q_pos[:, None] - c_pos[None, :]
    s = (jnp.einsum('btgrd,bcgd->btgrc', qg, kc[:, :, 0]).astype(F32)
         - slopes[:, :, None] * dc.astype(F32)[:, None, None, :])
    p_c = masked_softmax(s, (dc >= 0)[:, None, None, :])
    o_c = jnp.einsum('btgrc,bcgd->btgrd', p_c.astype(kc.dtype), kc[:, :, 1])
    ratio = SEL_BLOCK // CMP_STRIDE
    imp = jnp.sum(p_c, axis=3)
    imp = jnp.pad(imp, ((0, 0), (0, 0), (0, 0), (0, nsb * ratio - imp.shape[-1])))
    blk_imp = imp.reshape(B, Tq, NSA_KV_HEADS, nsb, ratio).sum(-1)
    straddle = imp[..., ratio - 1::ratio]
    blk_imp = blk_imp + jnp.pad(straddle[..., :-1], ((0, 0), (0, 0), (0, 0), (1, 0)))
    blk = jnp.arange(nsb)[None, :]
    cur = (q_pos // SEL_BLOCK)[:, None]
    valid = blk <= cur
    forced = valid & ((blk == 0) | (blk >= cur - 1))
    score = jnp.where(valid[None, :, None, :], blk_imp + FORCE_BONUS * forced[None, :, None, :], -jnp.inf)
    top, idx = lax.top_k(score, min(SEL_TOP, nsb))
    ok = jnp.isfinite(top)
    kv_s = gather(idx)
    n = idx.shape[-1]
    kpos = idx[..., None] * SEL_BLOCK + jnp.arange(SEL_BLOCK)
    ds = q_pos[None, :, None, None, None] - kpos
    s = (jnp.einsum('btgrd,btgnkd->btgrnk', qg, kv_s[..., 0, :]).astype(F32)
         - slopes[None, None, :, :, None, None] * ds.astype(F32)[:, :, :, None])
    m = (ok[..., None] & (ds >= 0))[:, :, :, None]
    p_s = masked_softmax(s.reshape(B, Tq, NSA_KV_HEADS, NSA_GROUP, n * SEL_BLOCK),
                         m.reshape(B, Tq, NSA_KV_HEADS, 1, n * SEL_BLOCK))
    v_s = kv_s[..., 1, :].reshape(B, Tq, NSA_KV_HEADS, n * SEL_BLOCK, NSA_HEAD_DIM)
    o_s = jnp.einsum('btgrk,btgkd->btgrd', p_s.astype(v_s.dtype), v_s)
    dw = q_pos[:, None] - kw_pos[None, :]
    mw = (dw >= 0) & (dw < WINDOW) & (kw_pos >= 0)[None, :]
    s = (jnp.einsum('btgrd,bsgd->btgrs', qg, kw[:, :, 0]).astype(F32)
         - slopes[:, :, None] * dw.astype(F32)[:, None, None, :])
    p_w = masked_softmax(s, mw[:, None, None, :])
    o_w = jnp.einsum('btgrs,bsgd->btgrd', p_w.astype(kw.dtype), kw[:, :, 1])
    o = gates[..., 0:1] * o_c + gates[..., 1:2] * o_s + gates[..., 2:3] * o_w
    return o.reshape(B, Tq, NSA_HEADS * NSA_HEAD_DIM)


def even_project(h, pos, w_in, g_q, w_q_up, g_kv, w_kv_up):
    B, T, _ = h.shape
    sizes = [MLA_Q_LORA, MLA_KV_LORA, MLA_ROPE, NSA_HEADS * NSA_HEAD_DIM, NSA_KV_W, NSA_KV_W, NSA_KV_W, 3 * NSA_HEADS]
    cuts = [int(c) for c in np.cumsum(sizes)[:-1]]
    cq, ckv, kr, nq, kvc, kvs, kvw, gt = jnp.split(h @ w_in, cuts, axis=-1)
    q = jnp.einsum('btc,chd->bthd', rmsnorm(cq, g_q), w_q_up)
    q_abs = jnp.einsum('bthn,chn->bthc', q[..., :MLA_NOPE], w_kv_up[..., :MLA_NOPE]) * MLA_SCALE
    q_rope = rope(q[..., MLA_NOPE:], pos) * MLA_SCALE
    ckv_rows = jnp.concatenate([rmsnorm(ckv, g_kv), rope(kr, pos)], axis=-1)
    grp = (B, T, NSA_KV_HEADS, NSA_GROUP)
    nq = nq.reshape(grp + (NSA_HEAD_DIM,)) * NSA_SCALE
    kv_shape = (B, T, 2, NSA_KV_HEADS, NSA_HEAD_DIM)
    gates = jax.nn.sigmoid(gt.reshape(grp + (3,)))
    return (q_abs, q_rope, ckv_rows, nq, kvc.reshape(kv_shape), kvs.reshape(kv_shape),
            kvw.reshape(kv_shape), gates)


def even_mixer_prompt(h, w_in, g_q, w_q_up, g_kv, w_kv_up, w_cmp, w_o):
    B, T, _ = h.shape
    pos = jnp.arange(T)
    q_abs, q_rope, ckv, nq, kvc, kvs, kvw, gates = even_project(h, pos, w_in, g_q, w_q_up, g_kv, w_kv_up)
    kc, c_pos = compressed_blocks(*chunk_partials(kvc, w_cmp))
    nsb = -(-T // SEL_BLOCK)
    sel_blocks = pad_rows(kvs, nsb * SEL_BLOCK).reshape((B, nsb, SEL_BLOCK) + kvs.shape[2:])
    kw_pad = jnp.pad(kvw, ((0, 0), (WINDOW, 0), (0, 0), (0, 0), (0, 0)))
    nqb = T // Q_BLOCK

    def to_blocks(a):
        return jnp.moveaxis(a.reshape((B, nqb, Q_BLOCK) + a.shape[2:]), 1, 0)

    def block(args):
        qa, qr, qn, gt, s0 = args
        q_pos = s0 + jnp.arange(Q_BLOCK)
        o_m = mla_up(mla_attend(qa, qr, [(ckv, pos)], q_pos), w_kv_up)
        kw = lax.dynamic_slice_in_dim(kw_pad, s0, Q_BLOCK + WINDOW, axis=1)
        kw_pos = s0 - WINDOW + jnp.arange(Q_BLOCK + WINDOW)
        o_n = nsa_core(qn, gt, q_pos, kc, c_pos, nsb, lambda idx: gather_blocks(sel_blocks, idx), kw, kw_pos)
        return jnp.concatenate([o_m, o_n], axis=-1)

    starts = jnp.arange(nqb, dtype=jnp.int32) * Q_BLOCK
    o = lax.map(block, (to_blocks(q_abs), to_blocks(q_rope), to_blocks(nq), to_blocks(gates), starts))
    o = jnp.moveaxis(o, 0, 1).reshape(B, T, MIX_WIDTH)
    return o @ w_o, ckv, kvc, kvs, kvw[:, T - min(WINDOW, T):]


def even_mixer_sample(h, page_table, c_mla, c_cmp, c_sel, c_win, w_in, g_q, w_q_up, g_kv, w_kv_up, w_cmp, w_o):
    DB, DS, _ = h.shape
    past = page_table.shape[1] * PAGE_SIZE
    pos_past = jnp.arange(past)
    pos_new = past + jnp.arange(DS)
    q_abs, q_rope, ckv_new, nq, kvc, kvs, kvw, gates = even_project(h, pos_new, w_in, g_q, w_q_up, g_kv, w_kv_up)
    ckv_past = c_mla[page_table].reshape(DB, past, MLA_KV_LORA + MLA_ROPE)
    fp, sp = chunk_partials(c_cmp[page_table].reshape((DB, past) + c_cmp.shape[2:]), w_cmp)
    fn, sn = chunk_partials(kvc, w_cmp)
    kc, c_pos = compressed_blocks(jnp.concatenate([fp, fn], axis=1), jnp.concatenate([sp, sn], axis=1))
    nsb = -(-(past + DS) // SEL_BLOCK)
    npb = past // SEL_BLOCK
    nnb = nsb - npb
    new_blocks = pad_rows(kvs, nnb * SEL_BLOCK).reshape((DB, nnb, SEL_BLOCK) + kvs.shape[2:])

    def gather(idx):
        old = gather_pool_blocks(c_sel, page_table, jnp.minimum(idx, npb - 1))
        new = gather_blocks(new_blocks, jnp.clip(idx - npb, 0, nnb - 1))
        return jnp.where((idx >= npb)[..., None, None, None], new, old)

    n_win = c_win.shape[1]
    kw = jnp.concatenate([c_win, kvw], axis=1)
    kw_pos = past - n_win + jnp.arange(n_win + DS)

    def to_steps(a):
        return jnp.moveaxis(a, 1, 0)[:, :, None]

    def step(args):
        qa, qr, qn, gt, t = args
        q_pos = past + t + jnp.arange(1)
        o_m = mla_up(mla_attend(qa, qr, [(ckv_past, pos_past), (ckv_new, pos_new)], q_pos), w_kv_up)
        o_n = nsa_core(qn, gt, q_pos, kc, c_pos, nsb, gather, kw, kw_pos)
        return jnp.concatenate([o_m, o_n], axis=-1)

    o = lax.map(step, (to_steps(q_abs), to_steps(q_rope), to_steps(nq), to_steps(gates),
                       jnp.arange(DS, dtype=jnp.int32)))
    o = jnp.moveaxis(o[:, :, 0], 0, 1)
    return o @ w_o, ckv_new, kvc, kvs, kw[:, DS:]


def gated_delta(q, k, v, g, beta, S0):
    B, T, H, _ = q.shape
    DV = v.shape[-1]
    C = min(DN_CHUNK, T)
    Tp = T + (-T) % C
    n = Tp // C

    def prep(x):
        x = pad_rows(x.astype(F32), Tp)
        return jnp.moveaxis(x.reshape((B, n, C) + x.shape[2:]), 3, 1)

    q, k, v, g, beta = prep(q), prep(k), prep(v), prep(g), prep(beta)
    gc = jnp.cumsum(g, axis=-1)
    i = jnp.arange(C)
    causal = i[:, None] >= i[None, :]
    strict = i[:, None] > i[None, :]
    decay = jnp.exp(jnp.where(causal, gc[..., :, None] - gc[..., None, :], -jnp.inf))
    kb = k * beta[..., None]
    lower = jnp.einsum('bhnid,bhnjd->bhnij', kb, k) * decay * strict
    a_mat = lower + jnp.eye(C, dtype=F32)
    rhs = jnp.concatenate([v * beta[..., None], kb * jnp.exp(gc)[..., None]], axis=-1)
    sol = lax.linalg.triangular_solve(a_mat, rhs, left_side=True, lower=True, unit_diagonal=True)
    u, w = sol[..., :DV], sol[..., DV:]
    attn = jnp.einsum('bhnid,bhnjd->bhnij', q, k) * decay
    qd = q * jnp.exp(gc)[..., None]
    kd = k * jnp.exp(gc[..., -1:] - gc)[..., None]
    gl = jnp.exp(gc[..., -1])

    def step(S, xs):
        u_c, w_c, qd_c, kd_c, at_c, gl_c = xs
        v_new = u_c - jnp.einsum('bhcd,bhde->bhce', w_c, S)
        o = jnp.einsum('bhcd,bhde->bhce', qd_c, S) + jnp.einsum('bhij,bhje->bhie', at_c, v_new)
        S = S * gl_c[..., None, None] + jnp.einsum('bhcd,bhce->bhde', kd_c, v_new)
        return S, o

    xs = tuple(jnp.moveaxis(a, 2, 0) for a in (u, w, qd, kd, attn, gl))
    S, o = lax.scan(step, S0.astype(F32), xs)
    o = jnp.moveaxis(o, 0, 2).reshape(B, H, Tp, DV)[:, :, :T]
    return jnp.transpose(o, (0, 2, 1, 3)), S


def delta_mixer(h, conv_buf, S0, w_in, w_conv, a_log, dt_bias, g_norm, w_out):
    B, T, _ = h.shape
    vd = DN_V_HEADS * DN_V_DIM
    proj = h @ w_in
    qkv = proj[..., :DN_CONV_DIM]
    z = proj[..., DN_CONV_DIM:DN_CONV_DIM + vd]
    b = proj[..., DN_CONV_DIM + vd:DN_CONV_DIM + vd + DN_V_HEADS]
    a = proj[..., DN_CONV_DIM + vd + DN_V_HEADS:]
    xp = jnp.concatenate([conv_buf.astype(qkv.dtype), qkv], axis=1)
    conv = xp[:, 0:T] * w_conv[0]
    for j in range(1, DN_CONV):
        conv = conv + xp[:, j:j + T] * w_conv[j]
    conv = jax.nn.silu(conv)
    qd = DN_QK_HEADS * DN_K_DIM
    rep = DN_V_HEADS // DN_QK_HEADS
    q = jnp.repeat(l2norm(conv[..., :qd].reshape(B, T, DN_QK_HEADS, DN_K_DIM)), rep, axis=2) * DN_K_DIM ** -0.5
    k = jnp.repeat(l2norm(conv[..., qd:2 * qd].reshape(B, T, DN_QK_HEADS, DN_K_DIM)), rep, axis=2)
    v = conv[..., 2 * qd:].reshape(B, T, DN_V_HEADS, DN_V_DIM)
    beta = jax.nn.sigmoid(b.astype(F32))
    g = -jnp.exp(a_log.astype(F32)) * jax.nn.softplus(a.astype(F32) + dt_bias.astype(F32))
    o, S = gated_delta(q, k, v, g, beta, S0)
    o = rmsnorm(o, g_norm) * jax.nn.silu(z.reshape(B, T, DN_V_HEADS, DN_V_DIM).astype(F32))
    return o.astype(h.dtype).reshape(B, T, vd) @ w_out, xp[:, T:], S


def setup_inputs(seed: int = 0) -> dict:
    key = jax.random.key(seed)
    ks = iter(jax.random.split(key, 40))

    def nrm(shape, scale):
        return jax.random.normal(next(ks), shape, F32) * scale

    def gain(shape):
        return 1.0 + 0.1 * jax.random.normal(next(ks), shape, F32)

    n_pages = PAST_LEN // PAGE_SIZE
    n_pool = (5 * DEC_BATCH * n_pages) // 4
    win_buf = min(WINDOW, PAST_LEN)
    kvh = (2, NSA_KV_HEADS, NSA_HEAD_DIM)
    x_prompt = nrm((BATCH, SEQ, D_MODEL), 1.0)
    x_sample = nrm((DEC_BATCH, DEC_SEQ, D_MODEL), 1.0)
    cache_mla = nrm((N_EVEN, n_pool, PAGE_SIZE, MLA_KV_LORA + MLA_ROPE), 1.0)
    cache_nsa_cmp = nrm((N_EVEN, n_pool, PAGE_SIZE) + kvh, 1.0)
    cache_nsa_sel = nrm((N_EVEN, n_pool, PAGE_SIZE) + kvh, 1.0)
    cache_nsa_win = nrm((N_EVEN, DEC_BATCH, win_buf) + kvh, 1.0)
    state_delta = nrm((N_ODD, DEC_BATCH, DN_V_HEADS, DN_K_DIM, DN_V_DIM), 0.1)
    state_delta_conv = nrm((N_ODD, DEC_BATCH, DN_CONV - 1, DN_CONV_DIM), 1.0)
    page_table = jax.random.permutation(next(ks), n_pool)[:DEC_BATCH * n_pages].reshape(DEC_BATCH, n_pages).astype(jnp.int32)
    return {
        'x_prompt': x_prompt,
        'x_sample': x_sample,
        'cache_mla': cache_mla,
        'cache_nsa_cmp': cache_nsa_cmp,
        'cache_nsa_sel': cache_nsa_sel,
        'cache_nsa_win': cache_nsa_win,
        'state_delta': state_delta,
        'state_delta_conv': state_delta_conv,
        'page_table': page_table,
        'norm_g': gain((DEPTH, 6, D_MODEL)),
        'w_ffn_in': nrm((DEPTH, 2, D_MODEL, 2 * D_FF), D_MODEL ** -0.5),
        'w_ffn_out': nrm((DEPTH, 2, D_FF, D_MODEL), D_FF ** -0.5),
        'w_in_even': nrm((N_EVEN, D_MODEL, EVEN_IN), D_MODEL ** -0.5),
        'g_q_lora': gain((N_EVEN, MLA_Q_LORA)),
        'w_q_up': nrm((N_EVEN, MLA_Q_LORA, MLA_HEADS, MLA_NOPE + MLA_ROPE), MLA_Q_LORA ** -0.5),
        'g_kv_lora': gain((N_EVEN, MLA_KV_LORA)),
        'w_kv_up': nrm((N_EVEN, MLA_KV_LORA, MLA_HEADS, MLA_NOPE + MLA_V), MLA_KV_LORA ** -0.5),
        'w_cmp': nrm((N_EVEN, 2, CMP_BLOCK, NSA_KV_HEADS), CMP_BLOCK ** -0.5),
        'w_o_even': nrm((N_EVEN, MIX_WIDTH, D_MODEL), MIX_WIDTH ** -0.5),
        'w_in_odd': nrm((N_ODD, D_MODEL, ODD_IN), D_MODEL ** -0.5),
        'w_conv': nrm((N_ODD, DN_CONV, DN_CONV_DIM), DN_CONV ** -0.5),
        'a_log': jnp.log(jax.random.uniform(next(ks), (N_ODD, DN_V_HEADS), F32, 1.0, 16.0)),
        'dt_bias': nrm((N_ODD, DN_V_HEADS), 0.1),
        'g_dn_norm': gain((N_ODD, DN_V_DIM)),
        'w_out_odd': nrm((N_ODD, DN_V_HEADS * DN_V_DIM, D_MODEL), (DN_V_HEADS * DN_V_DIM) ** -0.5),
    }


def reference(x_prompt, x_sample, cache_mla, cache_nsa_cmp, cache_nsa_sel, cache_nsa_win, state_delta,
              state_delta_conv, page_table, norm_g, w_ffn_in, w_ffn_out, w_in_even, g_q_lora, w_q_up,
              g_kv_lora, w_kv_up, w_cmp, w_o_even, w_in_odd, w_conv, a_log, dt_bias, g_dn_norm, w_out_odd):
    xp, xs = x_prompt, x_sample
    B = xp.shape[0]
    mla_p, mla_s, cmp_p, cmp_s, sel_p, sel_s, win_p, win_s = [], [], [], [], [], [], [], []
    dst_p, dst_s, dcv_p, dcv_s = [], [], [], []
    for l in range(DEPTH):
        ng = norm_g[l]
        i = l // 2
        xp = ffn_half(xp, ng[0], ng[1], w_ffn_in[l, 0], w_ffn_out[l, 0])
        xs = ffn_half(xs, ng[0], ng[1], w_ffn_in[l, 0], w_ffn_out[l, 0])
        hp = rmsnorm(xp, ng[2])
        hs = rmsnorm(xs, ng[2])
        if l % 2 == 0:
            ew = (w_in_even[i], g_q_lora[i], w_q_up[i], g_kv_lora[i], w_kv_up[i], w_cmp[i], w_o_even[i])
            yp, r_mla, r_cmp, r_sel, r_win = even_mixer_prompt(hp, *ew)
            ys, s_mla, s_cmp, s_sel, s_win = even_mixer_sample(hs, page_table, cache_mla[i], cache_nsa_cmp[i],
                                                               cache_nsa_sel[i], cache_nsa_win[i], *ew)
            mla_p.append(r_mla); cmp_p.append(r_cmp); sel_p.append(r_sel); win_p.append(r_win)
            mla_s.append(s_mla); cmp_s.append(s_cmp); sel_s.append(s_sel); win_s.append(s_win)
        else:
            dw = (w_in_odd[i], w_conv[i], a_log[i], dt_bias[i], g_dn_norm[i], w_out_odd[i])
            buf0 = jnp.zeros((B, DN_CONV - 1, DN_CONV_DIM), hp.dtype)
            S0 = jnp.zeros((B, DN_V_HEADS, DN_K_DIM, DN_V_DIM), F32)
            yp, c_p, S_p = delta_mixer(hp, buf0, S0, *dw)
            ys, c_s, S_s = delta_mixer(hs, state_delta_conv[i], state_delta[i], *dw)
            dst_p.append(S_p.astype(state_delta.dtype)); dst_s.append(S_s.astype(state_delta.dtype))
            dcv_p.append(c_p); dcv_s.append(c_s)
        xp = xp + rmsnorm(yp, ng[3])
        xs = xs + rmsnorm(ys, ng[3])
        xp = ffn_half(xp, ng[4], ng[5], w_ffn_in[l, 1], w_ffn_out[l, 1])
        xs = ffn_half(xs, ng[4], ng[5], w_ffn_in[l, 1], w_ffn_out[l, 1])
    return (xp, xs,
            jnp.stack(mla_p), jnp.stack(mla_s),
            jnp.stack(cmp_p), jnp.stack(cmp_s),
            jnp.stack(sel_p), jnp.stack(sel_s),
            jnp.stack(win_p), jnp.stack(win_s),
            jnp.stack(dst_p), jnp.stack(dst_s),
            jnp.stack(dcv_p), jnp.stack(dcv_s))
```

```python
import functools

import jax
import jax.numpy as jnp
import numpy as np
from jax import lax
from jax.experimental import pallas as pl
from jax.experimental.pallas import tpu as pltpu

D_MODEL = 2048
DEPTH = 4
PAGE_SIZE = 128
MLA_HEADS = 8
MLA_Q_LORA = 512
MLA_KV_LORA = 256
MLA_NOPE = 128
MLA_ROPE = 64
MLA_V = 128
MLA_ROW = MLA_KV_LORA + MLA_ROPE
ROPE_THETA = 10000.0
MLA_SCALE = (MLA_NOPE + MLA_ROPE) ** -0.5
NSA_HEADS = 16
NSA_KV_HEADS = 2
NSA_GROUP = NSA_HEADS // NSA_KV_HEADS
NSA_HEAD_DIM = 64
NSA_SCALE = NSA_HEAD_DIM ** -0.5
CMP_STRIDE = 16
CMP_BLOCK = 2 * CMP_STRIDE
SEL_BLOCK = 64
SEL_TOP = 16
SEL_RATIO = SEL_BLOCK // CMP_STRIDE
WINDOW = 512
FORCE_BONUS = 1.0e4
DN_QK_HEADS = 16
DN_V_HEADS = 32
DN_K_DIM = 128
DN_V_DIM = 128
DN_CONV = 4
DN_CHUNK = 64
DN_CONV_DIM = 2 * DN_QK_HEADS * DN_K_DIM + DN_V_HEADS * DN_V_DIM
D_FF = 5632
NORM_EPS = 1e-6
L2_EPS = 1e-6
NEG_BIG = -1e30
TINY = 1e-30
NSA_KV_W = 2 * NSA_KV_HEADS * NSA_HEAD_DIM
NSA_KV_HALF = NSA_KV_HEADS * NSA_HEAD_DIM

F32 = jnp.float32
BF16 = jnp.bfloat16

VMEM_LIMIT_BYTES = 56 * 1024 * 1024
ATT_TILE = 256

_NT = (((1,), (1,)), ((), ()))


def _params(*sem):
    return pltpu.CompilerParams(dimension_semantics=sem, vmem_limit_bytes=VMEM_LIMIT_BYTES)


def _ffn_kernel(x_ref, gpre_ref, gpost_ref, wg_ref, wu_ref, wo_ref, o_ref, h_scr, acc_scr):
    j = pl.program_id(1)

    @pl.when(j == 0)
    def _():
        x = x_ref[...]
        ms = jnp.mean(x * x, axis=-1, keepdims=True)
        h_scr[...] = (x * lax.rsqrt(ms + NORM_EPS) * gpre_ref[...]).astype(BF16)
        acc_scr[...] = jnp.zeros_like(acc_scr)

    h = h_scr[...]
    gate = jnp.dot(h, wg_ref[...], preferred_element_type=F32)
    up = jnp.dot(h, wu_ref[...], preferred_element_type=F32)
    a = (gate * jax.nn.sigmoid(gate) * up).astype(BF16)
    acc_scr[...] += jnp.dot(a, wo_ref[...], preferred_element_type=F32)

    @pl.when(j == pl.num_programs(1) - 1)
    def _():
        y = acc_scr[...]
        ms = jnp.mean(y * y, axis=-1, keepdims=True)
        o_ref[...] = x_ref[...] + 0.5 * (y * lax.rsqrt(ms + NORM_EPS) * gpost_ref[...])


def ffn_half(x, g_pre, g_post, w_in, w_out, tm=512, tf=512):
    m, d = x.shape
    f = w_out.shape[0]
    nf = f // tf
    assert m % tm == 0 and f % tf == 0
    return pl.pallas_call(
        _ffn_kernel,
        grid=(m // tm, nf),
        in_specs=[
            pl.BlockSpec((tm, d), lambda i, j: (i, 0)),
            pl.BlockSpec((1, d), lambda i, j: (0, 0)),
            pl.BlockSpec((1, d), lambda i, j: (0, 0)),
            pl.BlockSpec((d, tf), lambda i, j: (0, j)),
            pl.BlockSpec((d, tf), lambda i, j: (0, j + nf)),
            pl.BlockSpec((tf, d), lambda i, j: (j, 0)),
        ],
        out_specs=pl.BlockSpec((tm, d), lambda i, j: (i, 0)),
        out_shape=jax.ShapeDtypeStruct((m, d), F32),
        scratch_shapes=[pltpu.VMEM((tm, d), BF16), pltpu.VMEM((tm, d), F32)],
        compiler_params=_params("arbitrary", "arbitrary"),
        name="ffn_half",
    )(x, g_pre.reshape(1, d), g_post.reshape(1, d), w_in, w_in, w_out)


def _norm_proj_kernel(x_ref, g_ref, w_ref, o_ref, h_scr):
    @pl.when(pl.program_id(1) == 0)
    def _():
        x = x_ref[...]
        ms = jnp.mean(x * x, axis=-1, keepdims=True)
        h_scr[...] = (x * lax.rsqrt(ms + NORM_EPS) * g_ref[...]).astype(BF16)

    o_ref[...] = jnp.dot(h_scr[...], w_ref[...], preferred_element_type=F32)


def norm_proj(x, g, w, tm, tn):
    m, d = x.shape
    n = w.shape[1]
    assert m % tm == 0 and n % tn == 0
    return pl.pallas_call(
        _norm_proj_kernel,
        grid=(m // tm, n // tn),
        in_specs=[
            pl.BlockSpec((tm, d), lambda i, j: (i, 0)),
            pl.BlockSpec((1, d), lambda i, j: (0, 0)),
            pl.BlockSpec((d, tn), lambda i, j: (0, j)),
        ],
        out_specs=pl.BlockSpec((tm, tn), lambda i, j: (i, j)),
        out_shape=jax.ShapeDtypeStruct((m, n), F32),
        scratch_shapes=[pltpu.VMEM((tm, d), BF16)],
        compiler_params=_params("arbitrary", "arbitrary"),
        name="norm_proj",
    )(x, g.reshape(1, d), w)


def _out_proj_kernel(a_ref, w_ref, x_ref, g_ref, o_ref):
    y = jnp.dot(a_ref[...].astype(BF16), w_ref[...], preferred_element_type=F32)
    ms = jnp.mean(y * y, axis=-1, keepdims=True)
    o_ref[...] = x_ref[...] + y * lax.rsqrt(ms + NORM_EPS) * g_ref[...]


def out_proj_residual(a, w, x, g, tm=256):
    m, k = a.shape
    d = w.shape[1]
    assert m % tm == 0
    return pl.pallas_call(
        _out_proj_kernel,
        grid=(m // tm,),
        in_specs=[
            pl.BlockSpec((tm, k), lambda i: (i, 0)),
            pl.BlockSpec((k, d), lambda i: (0, 0)),
            pl.BlockSpec((tm, d), lambda i: (i, 0)),
            pl.BlockSpec((1, d), lambda i: (0, 0)),
        ],
        out_specs=pl.BlockSpec((tm, d), lambda i: (i, 0)),
        out_shape=jax.ShapeDtypeStruct((m, d), F32),
        compiler_params=_params("arbitrary"),
        name="out_proj_residual",
    )(a, w, x, g.reshape(1, d))


def rmsnorm(x, g):
    xf = x.astype(F32)
    y = xf * lax.rsqrt(jnp.mean(xf * xf, axis=-1, keepdims=True) + NORM_EPS)
    return (y * g.astype(F32)).astype(x.dtype)


def l2norm(x):
    xf = x.astype(F32)
    return xf * lax.rsqrt(jnp.sum(xf * xf, axis=-1, keepdims=True) + L2_EPS)


def rope_rows(x, pos):
    half = x.shape[-1] // 2
    inv = ROPE_THETA ** (-2.0 * jnp.arange(half, dtype=F32) / x.shape[-1])
    ang = pos.astype(F32)[:, None] * inv
    shape = (pos.shape[0],) + (1,) * (x.ndim - 2) + (half,)
    cos = jnp.cos(ang).reshape(shape)
    sin = jnp.sin(ang).reshape(shape)
    x1 = x[..., :half]
    x2 = x[..., half:]
    return jnp.concatenate([x1 * cos - x2 * sin, x1 * sin + x2 * cos], axis=-1)


def alibi_slopes():
    h = jnp.arange(NSA_HEADS, dtype=F32) + 1.0
    return (2.0 ** (-8.0 * h / NSA_HEADS)).reshape(NSA_KV_HEADS, NSA_GROUP)


def pad_rows(x, length):
    return jnp.pad(x, [(0, 0), (0, length - x.shape[1])] + [(0, 0)] * (x.ndim - 2))


def _masked_softmax_parts(s, valid):
    s = jnp.where(valid, s, NEG_BIG)
    m = jnp.max(s, axis=-1, keepdims=True)
    e = jnp.where(valid, jnp.exp(s - m), 0.0)
    return e, jnp.sum(e, axis=-1, keepdims=True)


def _topk_mask(score, k):
    n = score.shape[-1]
    lane = lax.broadcasted_iota(jnp.int32, score.shape, score.ndim - 1)
    sel = jnp.zeros(score.shape, F32)
    x = score
    for _ in range(k):
        m = jnp.max(x, axis=-1, keepdims=True)
        idx = jnp.min(jnp.where(x == m, lane, n), axis=-1, keepdims=True)
        hit = lane == idx
        sel = jnp.where(hit & (m > -jnp.inf), 1.0, sel)
        x = jnp.where(hit, -jnp.inf, x)
    return sel


def _block_scores(blk_imp, q_pos):
    blk = lax.broadcasted_iota(jnp.int32, blk_imp.shape, 1)
    cur = q_pos // SEL_BLOCK
    valid = blk <= cur
    forced = valid & ((blk == 0) | (blk >= cur - 1))
    return jnp.where(valid, blk_imp + FORCE_BONUS * forced.astype(F32), -jnp.inf)


def _imp_to_block_matrix(n_cmp, n_blk):
    c = np.arange(n_cmp)[:, None]
    n = np.arange(n_blk)[None, :]
    return jnp.asarray(((c // SEL_RATIO == n) | (c == SEL_RATIO * n - 1)).astype(np.float32))


def _mla_prompt_kernel(q_ref, k_ref, wup_ref, o_ref, m_scr, l_scr, acc_scr):
    i = pl.program_id(1)
    j = pl.program_id(2)
    heads, tq, dq = q_ref.shape
    tk = k_ref.shape[0]

    @pl.when(j == 0)
    def _():
        m_scr[...] = jnp.full_like(m_scr, NEG_BIG)
        l_scr[...] = jnp.zeros_like(l_scr)
        acc_scr[...] = jnp.zeros_like(acc_scr)

    @pl.when(j <= i)
    def _():
        q = q_ref[...].reshape(heads * tq, dq)
        k = k_ref[...]
        s = lax.dot_general(q, k, _NT, preferred_element_type=F32)
        row = lax.broadcasted_iota(jnp.int32, (heads * tq, 1), 0)
        t_pos = i * tq + (row & (tq - 1))
        k_pos = j * tk + lax.broadcasted_iota(jnp.int32, (1, tk), 1)
        valid = k_pos <= t_pos
        s = jnp.where(valid, s, NEG_BIG)
        m_old = m_scr[...]
        m_new = jnp.maximum(m_old, jnp.max(s, axis=-1, keepdims=True))
        e = jnp.where(valid, jnp.exp(s - m_new), 0.0)
        alpha = jnp.exp(m_old - m_new)
        l_scr[...] = alpha * l_scr[...] + jnp.sum(e, axis=-1, keepdims=True)
        acc_scr[...] = alpha * acc_scr[...] + jnp.dot(e.astype(BF16), k[:, :MLA_KV_LORA], preferred_element_type=F32)
        m_scr[...] = m_new

    @pl.when(j == i)
    def _():
        o_lat = (acc_scr[...] / jnp.maximum(l_scr[...], TINY)).astype(BF16)
        outs = [jnp.dot(o_lat[h * tq:(h + 1) * tq], wup_ref[h], preferred_element_type=F32) for h in range(heads)]
        o_ref[...] = jnp.concatenate(outs, axis=-1)


def mla_prompt(q_cat, ckv, w_up, tile=ATT_TILE):
    b, h, t, dq = q_cat.shape
    assert t % tile == 0 and tile & (tile - 1) == 0
    nt = t // tile
    return pl.pallas_call(
        _mla_prompt_kernel,
        grid=(b, nt, nt),
        in_specs=[
            pl.BlockSpec((None, h, tile, dq), lambda b_, i, j: (b_, 0, i, 0)),
            pl.BlockSpec((None, tile, dq), lambda b_, i, j: (b_, jnp.minimum(i, j), 0)),
            pl.BlockSpec((h, MLA_KV_LORA, MLA_V), lambda b_, i, j: (0, 0, 0)),
        ],
        out_specs=pl.BlockSpec((None, tile, h * MLA_V), lambda b_, i, j: (b_, i, 0)),
        out_shape=jax.ShapeDtypeStruct((b, t, h * MLA_V), F32),
        scratch_shapes=[pltpu.VMEM((h * tile, 1), F32), pltpu.VMEM((h * tile, 1), F32),
                        pltpu.VMEM((h * tile, MLA_KV_LORA), F32)],
        compiler_params=_params("arbitrary", "arbitrary", "arbitrary"),
        name="mla_prompt",
    )(q_cat, ckv, w_up)


def _page_specs(n_pages, width):
    return [pl.BlockSpec((None, PAGE_SIZE, width), lambda b, pt, p=p: (pt[b, p], 0, 0)) for p in range(n_pages)]


def _mla_sample_kernel(pt_ref, q_ref, new_ref, wup_ref, *rest, n_pages, n_new):
    pages = rest[:n_pages]
    o_ref = rest[n_pages]
    s_scr = rest[n_pages + 1]
    q = q_ref[...]
    rows = q.shape[0]
    heads = rows // n_new
    for p in range(n_pages):
        kp = pages[p][...].astype(BF16)
        s_scr[:, p * PAGE_SIZE:(p + 1) * PAGE_SIZE] = lax.dot_general(q, kp, _NT, preferred_element_type=F32)
    new = new_ref[...]
    s_new = lax.dot_general(q, new, _NT, preferred_element_type=F32)
    t_row = lax.broadcasted_iota(jnp.int32, (rows, 1), 0) & (n_new - 1)
    valid_new = lax.broadcasted_iota(jnp.int32, (1, new.shape[0]), 1) <= t_row
    s_new = jnp.where(valid_new, s_new, NEG_BIG)
    s = s_scr[...]
    m = jnp.maximum(jnp.max(s, axis=-1, keepdims=True), jnp.max(s_new, axis=-1, keepdims=True))
    e = jnp.exp(s - m).astype(BF16)
    e_new = jnp.where(valid_new, jnp.exp(s_new - m), 0.0)
    l = jnp.sum(e.astype(F32), axis=-1, keepdims=True) + jnp.sum(e_new, axis=-1, keepdims=True)
    acc = jnp.dot(e_new.astype(BF16), new[:, :MLA_KV_LORA], preferred_element_type=F32)
    for p in range(n_pages):
        vp = pages[p][:, :MLA_KV_LORA].astype(BF16)
        acc = acc + jnp.dot(e[:, p * PAGE_SIZE:(p + 1) * PAGE_SIZE], vp, preferred_element_type=F32)
    o_lat = (acc / jnp.maximum(l, TINY)).astype(BF16)
    outs = [jnp.dot(o_lat[h * n_new:(h + 1) * n_new], wup_ref[h], preferred_element_type=F32) for h in range(heads)]
    o_ref[...] = jnp.concatenate(outs, axis=-1)


def mla_sample(page_table, q_cat, new_rows, w_up, cache):
    db, rows, dq = q_cat.shape
    n_pages = page_table.shape[1]
    heads = w_up.shape[0]
    n_new = rows // heads
    assert n_new & (n_new - 1) == 0 and n_new <= new_rows.shape[1]
    n_pad = new_rows.shape[1]
    grid_spec = pltpu.PrefetchScalarGridSpec(
        num_scalar_prefetch=1,
        grid=(db,),
        in_specs=[
            pl.BlockSpec((None, rows, dq), lambda b, pt: (b, 0, 0)),
            pl.BlockSpec((None, n_pad, dq), lambda b, pt: (b, 0, 0)),
            pl.BlockSpec((heads, MLA_KV_LORA, MLA_V), lambda b, pt: (0, 0, 0)),
        ] + _page_specs(n_pages, dq),
        out_specs=pl.BlockSpec((None, n_new, heads * MLA_V), lambda b, pt: (b, 0, 0)),
        scratch_shapes=[pltpu.VMEM((rows, n_pages * PAGE_SIZE), F32)],
    )
    return pl.pallas_call(
        functools.partial(_mla_sample_kernel, n_pages=n_pages, n_new=n_new),
        grid_spec=grid_spec,
        out_shape=jax.ShapeDtypeStruct((db, n_new, heads * MLA_V), F32),
        compiler_params=_params("arbitrary"),
        name="mla_sample",
    )(page_table, q_cat, new_rows, w_up, *([cache] * n_pages))


def _nsa_cmp_prompt_kernel(q_ref, kk_ref, kv_ref, slope_ref, a_ref, o_ref, sel_ref, *, n_cmp):
    i = pl.program_id(2)
    n_heads, tq, _ = q_ref.shape
    kk = kk_ref[...]
    kv = kv_ref[...]
    n_pad = kk.shape[0]
    t_pos = i * tq + lax.broadcasted_iota(jnp.int32, (tq, 1), 0)
    c_idx = lax.broadcasted_iota(jnp.int32, (1, n_pad), 1)
    dc = t_pos - (c_idx * CMP_STRIDE + (CMP_BLOCK - 1))
    valid = (dc >= 0) & (c_idx < n_cmp)
    dcf = dc.astype(F32)
    imp = jnp.zeros((tq, n_pad), F32)
    outs = []
    for r in range(n_heads):
        s = lax.dot_general(q_ref[r], kk, _NT, preferred_element_type=F32) - slope_ref[r] * dcf
        e, l = _masked_softmax_parts(s, valid)
        p = e / jnp.maximum(l, TINY)
        imp = imp + p
        outs.append(jnp.dot(p.astype(BF16), kv, preferred_element_type=F32))
    o_ref[...] = jnp.concatenate(outs, axis=-1)
    blk_imp = jnp.dot(imp, a_ref[...], preferred_element_type=F32, precision=lax.Precision.HIGHEST)
    sel_ref[...] = _topk_mask(_block_scores(blk_imp, t_pos), min(SEL_TOP, blk_imp.shape[-1]))


def nsa_cmp_prompt(q, kc_k, kc_v, slopes, n_cmp, tile=ATT_TILE):
    b, g, r, t, dh = q.shape
    n_pad = kc_k.shape[2]
    nsb = -(-t // SEL_BLOCK)
    a_mat = _imp_to_block_matrix(n_pad, nsb)
    return pl.pallas_call(
        functools.partial(_nsa_cmp_prompt_kernel, n_cmp=n_cmp),
        grid=(b, g, t // tile),
        in_specs=[
            pl.BlockSpec((None, None, r, tile, dh), lambda b_, g_, i: (b_, g_, 0, i, 0)),
            pl.BlockSpec((None, None, n_pad, dh), lambda b_, g_, i: (b_, g_, 0, 0)),
            pl.BlockSpec((None, None, n_pad, dh), lambda b_, g_, i: (b_, g_, 0, 0)),
            pl.BlockSpec((None, r, 1, 1), lambda b_, g_, i: (g_, 0, 0, 0)),
            pl.BlockSpec((n_pad, nsb), lambda b_, g_, i: (0, 0)),
        ],
        out_specs=[
            pl.BlockSpec((None, tile, r * dh), lambda b_, g_, i: (b_, i, g_)),
            pl.BlockSpec((None, None, tile, nsb), lambda b_, g_, i: (b_, g_, i, 0)),
        ],
        out_shape=[jax.ShapeDtypeStruct((b, t, g * r * dh), F32), jax.ShapeDtypeStruct((b, g, t, nsb), F32)],
        compiler_params=_params("arbitrary", "arbitrary", "arbitrary"),
        name="nsa_cmp_prompt",
    )(q, kc_k, kc_v, slopes.reshape(g, r, 1, 1), a_mat)


def _nsa_flash_kernel(q_ref, k_ref, v_ref, slope_ref, o_ref, m_scr, l_scr, acc_scr, *, n_band):
    i = pl.program_id(2)
    jj = pl.program_id(3)
    n_heads, tq, _ = q_ref.shape
    tk = k_ref.shape[0]
    j = jj if n_band == 0 else i - (n_band - 1) + jj
    active = (j <= i) if n_band == 0 else (j >= 0)

    @pl.when(jj == 0)
    def _():
        m_scr[...] = jnp.full_like(m_scr, NEG_BIG)
        l_scr[...] = jnp.zeros_like(l_scr)
        acc_scr[...] = jnp.zeros_like(acc_scr)

    @pl.when(active)
    def _():
        k = k_ref[...]
        v = v_ref[...]
        t_pos = i * tq + lax.broadcasted_iota(jnp.int32, (tq, 1), 0)
        k_pos = j * tk + lax.broadcasted_iota(jnp.int32, (1, tk), 1)
        d = t_pos - k_pos
        in_range = (d >= 0) if n_band == 0 else ((d >= 0) & (d < WINDOW))
        df = d.astype(F32)
        for r in range(n_heads):
            s = lax.dot_general(q_ref[r], k, _NT, preferred_element_type=F32) - slope_ref[r] * df
            valid = in_range & (s > 0.1 * NEG_BIG)
            s = jnp.where(valid, s, NEG_BIG)
            m_old = m_scr[r]
            m_new = jnp.maximum(m_old, jnp.max(s, axis=-1, keepdims=True))
            e = jnp.where(valid, jnp.exp(s - m_new), 0.0)
            alpha = jnp.exp(m_old - m_new)
            l_scr[r] = alpha * l_scr[r] + jnp.sum(e, axis=-1, keepdims=True)
            acc_scr[r] = alpha * acc_scr[r] + jnp.dot(e.astype(BF16), v, preferred_element_type=F32)
            m_scr[r] = m_new

    @pl.when(jj == pl.num_programs(3) - 1)
    def _():
        outs = [acc_scr[r] / jnp.maximum(l_scr[r], TINY) for r in range(n_heads)]
        o_ref[...] = jnp.concatenate(outs, axis=-1)


def nsa_flash_prompt(q, k, v, slopes, n_band, tile=ATT_TILE):
    b, g, r, t, dq = q.shape
    dh = v.shape[-1]
    nt = t // tile
    if n_band == 0:
        n_steps = nt
        kv_map = lambda b_, g_, i, jj: (b_, g_, jnp.minimum(i, jj), 0)
    else:
        n_steps = n_band
        kv_map = lambda b_, g_, i, jj: (b_, g_, jnp.maximum(i - (n_band - 1) + jj, 0), 0)
    return pl.pallas_call(
        functools.partial(_nsa_flash_kernel, n_band=n_band),
        grid=(b, g, nt, n_steps),
        in_specs=[
            pl.BlockSpec((None, None, r, tile, dq), lambda b_, g_, i, jj: (b_, g_, 0, i, 0)),
            pl.BlockSpec((None, None, tile, dq), kv_map),
            pl.BlockSpec((None, None, tile, dh), kv_map),
            pl.BlockSpec((None, r, 1, 1), lambda b_, g_, i, jj: (g_, 0, 0, 0)),
        ],
        out_specs=pl.BlockSpec((None, tile, r * dh), lambda b_, g_, i, jj: (b_, i, g_)),
        out_shape=jax.ShapeDtypeStruct((b, t, g * r * dh), F32),
        scratch_shapes=[pltpu.VMEM((r, tile, 1), F32), pltpu.VMEM((r, tile, 1), F32), pltpu.VMEM((r, tile, dh), F32)],
        compiler_params=_params("arbitrary", "arbitrary", "arbitrary", "arbitrary"),
        name="nsa_flash_sel" if n_band == 0 else "nsa_flash_win",
    )(q, k, v, slopes.reshape(g, r, 1, 1))


def _nsa_cmp_sample_kernel(pt_ref, q_ref, new_ref, w_ref, slope_ref, a_ref, *rest, n_pages, n_new, past, nsb):
    pages = rest[:n_pages]
    o_ref, sel_ref = rest[n_pages:n_pages + 2]
    f_scr, s_scr = rest[n_pages + 2:]
    cpp = PAGE_SIZE // CMP_STRIDE
    nc = n_pages * cpp
    w1 = w_ref[0]
    w2 = w_ref[1]
    for p in range(n_pages):
        x = pages[p][...].reshape(cpp, CMP_STRIDE, NSA_KV_W)
        f_scr[p * cpp:(p + 1) * cpp] = jnp.sum(x * w1[None], axis=1)
        s_scr[p * cpp:(p + 1) * cpp] = jnp.sum(x * w2[None], axis=1)
    xn = new_ref[...]
    row8 = lax.broadcasted_iota(jnp.int32, (8, 1), 0)
    f_scr[nc:nc + 8] = jnp.where(row8 == 0, jnp.sum(xn * w1, axis=0, keepdims=True), 0.0)
    s_scr[nc:nc + 8] = jnp.where(row8 == 0, jnp.sum(xn * w2, axis=0, keepdims=True), 0.0)
    kc = f_scr[0:nc] + s_scr[1:nc + 1]
    kk = kc[:, :NSA_KV_HALF].astype(BF16)
    kv = kc[:, NSA_KV_HALF:].astype(BF16)
    q = q_ref[...]
    rows = q.shape[0]
    row = lax.broadcasted_iota(jnp.int32, (rows, 1), 0)
    q_pos = past + ((row // NSA_GROUP) & (n_new - 1))
    c_idx = lax.broadcasted_iota(jnp.int32, (1, nc), 1)
    dc = q_pos - (c_idx * CMP_STRIDE + (CMP_BLOCK - 1))
    valid = dc >= 0
    s = lax.dot_general(q, kk, _NT, preferred_element_type=F32) - slope_ref[...] * dc.astype(F32)
    e, l = _masked_softmax_parts(s, valid)
    p = e / jnp.maximum(l, TINY)
    o_ref[...] = jnp.dot(p.astype(BF16), kv, preferred_element_type=F32)
    imp = jnp.sum(p.reshape(rows // NSA_GROUP, NSA_GROUP, nc), axis=1)
    blk_imp = jnp.dot(imp, a_ref[...], preferred_element_type=F32, precision=lax.Precision.HIGHEST)
    row_g = lax.broadcasted_iota(jnp.int32, (rows // NSA_GROUP, 1), 0)
    score = _block_scores(blk_imp, past + (row_g & (n_new - 1)))
    blk = lax.broadcasted_iota(jnp.int32, score.shape, 1)
    score = jnp.where(blk < nsb, score, -jnp.inf)
    sel_ref[...] = _topk_mask(score, min(SEL_TOP, nsb))


def nsa_cmp_sample(page_table, q_bd, new_rows, w12, slope_rows, cache, n_new):
    db, rows, _ = q_bd.shape
    n_pages = page_table.shape[1]
    past = n_pages * PAGE_SIZE
    nc = past // CMP_STRIDE
    nsb = -(-(past + n_new) // SEL_BLOCK)
    nsb_pad = -(-nsb // 128) * 128
    assert n_new <= CMP_STRIDE and new_rows.shape[1] == CMP_STRIDE and n_new & (n_new - 1) == 0
    a_mat = _imp_to_block_matrix(nc, nsb_pad)
    grid_spec = pltpu.PrefetchScalarGridSpec(
        num_scalar_prefetch=1,
        grid=(db,),
        in_specs=[
            pl.BlockSpec((None, rows, NSA_KV_HALF), lambda b, pt: (b, 0, 0)),
            pl.BlockSpec((None, CMP_STRIDE, NSA_KV_W), lambda b, pt: (b, 0, 0)),
            pl.BlockSpec((2, CMP_STRIDE, NSA_KV_W), lambda b, pt: (0, 0, 0)),
            pl.BlockSpec((rows, 1), lambda b, pt: (0, 0)),
            pl.BlockSpec((nc, nsb_pad), lambda b, pt: (0, 0)),
        ] + _page_specs(n_pages, NSA_KV_W),
        out_specs=[
            pl.BlockSpec((None, rows, NSA_KV_HALF), lambda b, pt: (b, 0, 0)),
            pl.BlockSpec((None, rows // NSA_GROUP, nsb_pad), lambda b, pt: (b, 0, 0)),
        ],
        scratch_shapes=[pltpu.VMEM((nc + 8, NSA_KV_W), F32), pltpu.VMEM((nc + 8, NSA_KV_W), F32)],
    )
    return pl.pallas_call(
        functools.partial(_nsa_cmp_sample_kernel, n_pages=n_pages, n_new=n_new, past=past, nsb=nsb),
        grid_spec=grid_spec,
        out_shape=[jax.ShapeDtypeStruct((db, rows, NSA_KV_HALF), F32),
                   jax.ShapeDtypeStruct((db, rows // NSA_GROUP, nsb_pad), F32)],
        compiler_params=_params("arbitrary"),
        name="nsa_cmp_sample",
    )(page_table, q_bd, new_rows, w12, slope_rows, a_mat, *([cache] * n_pages))


def _nsa_sel_sample_kernel(pt_ref, q_ref, selnew_ref, kmask_ref, news_ref, win_ref, neww_ref, slope_ref, oc_ref,
                           gate_ref, *rest, n_pages, n_new, past):
    pages = rest[:n_pages]
    o_ref = rest[n_pages]
    s_scr = rest[n_pages + 1]
    q2 = q_ref[...]
    qb = q2[:, :NSA_KV_HALF]
    rows = q2.shape[0]
    row = lax.broadcasted_iota(jnp.int32, (rows, 1), 0)
    q_pos = past + ((row // NSA_GROUP) & (n_new - 1))
    slope = slope_ref[...]
    for p in range(n_pages):
        kp = jnp.concatenate([pages[p][:, :NSA_KV_HALF].astype(BF16),
                              kmask_ref[p * PAGE_SIZE:(p + 1) * PAGE_SIZE]], axis=-1)
        s_scr[:, p * PAGE_SIZE:(p + 1) * PAGE_SIZE] = lax.dot_general(q2, kp, _NT, preferred_element_type=F32)
    k_pos = lax.broadcasted_iota(jnp.int32, (1, n_pages * PAGE_SIZE), 1)
    d = q_pos - k_pos
    s = s_scr[...] - slope * d.astype(F32)
    valid = (d >= 0) & (s > 0.1 * NEG_BIG)
    s = jnp.where(valid, s, NEG_BIG)
    news = news_ref[...]
    n_pad = news.shape[0]
    dn = q_pos - (past + lax.broadcasted_iota(jnp.int32, (1, n_pad), 1))
    valid_n = (dn >= 0) & (selnew_ref[...] > 0.5)
    s_n = (lax.dot_general(qb, news[:, :NSA_KV_HALF].astype(BF16), _NT, preferred_element_type=F32)
           - slope * dn.astype(F32))
    s_n = jnp.where(valid_n, s_n, NEG_BIG)
    m = jnp.maximum(jnp.max(s, axis=-1, keepdims=True), jnp.max(s_n, axis=-1, keepdims=True))
    e = jnp.where(valid, jnp.exp(s - m), 0.0)
    e_n = jnp.where(valid_n, jnp.exp(s_n - m), 0.0)
    l = jnp.sum(e, axis=-1, keepdims=True) + jnp.sum(e_n, axis=-1, keepdims=True)
    eb = e.astype(BF16)
    acc = jnp.dot(e_n.astype(BF16), news[:, NSA_KV_HALF:].astype(BF16), preferred_element_type=F32)
    for p in range(n_pages):
        vp = pages[p][:, NSA_KV_HALF:].astype(BF16)
        acc = acc + jnp.dot(eb[:, p * PAGE_SIZE:(p + 1) * PAGE_SIZE], vp, preferred_element_type=F32)
    o_s = acc / jnp.maximum(l, TINY)
    kw = jnp.concatenate([win_ref[...], neww_ref[...]], axis=0)
    n_win = win_ref.shape[0]
    kw_pos = past - n_win + lax.broadcasted_iota(jnp.int32, (1, kw.shape[0]), 1)
    dw = q_pos - kw_pos
    valid_w = (dw >= 0) & (dw < WINDOW) & (kw_pos >= 0)
    s_w = (lax.dot_general(qb, kw[:, :NSA_KV_HALF].astype(BF16), _NT, preferred_element_type=F32)
           - slope * dw.astype(F32))
    e_w, l_w = _masked_softmax_parts(s_w, valid_w)
    p_w = e_w / jnp.maximum(l_w, TINY)
    o_w = jnp.dot(p_w.astype(BF16), kw[:, NSA_KV_HALF:].astype(BF16), preferred_element_type=F32)
    gate = gate_ref[...]
    o = gate[:, 0:1] * oc_ref[...] + gate[:, 1:2] * o_s + gate[:, 2:3] * o_w
    o_ref[...] = jnp.where(row < rows // NSA_KV_HEADS, o[:, :NSA_HEAD_DIM], o[:, NSA_HEAD_DIM:])


def nsa_sel_sample(page_table, q2, selnew, kmask, news, win, neww, slope_rows, o_c, gates, cache, n_new):
    db, rows, _ = q2.shape
    n_pages = page_table.shape[1]
    past = n_pages * PAGE_SIZE
    n_win = win.shape[1]
    n_pad = news.shape[1]
    per_seq = lambda *shape: pl.BlockSpec((None,) + shape, lambda b, pt: (b,) + (0,) * len(shape))
    grid_spec = pltpu.PrefetchScalarGridSpec(
        num_scalar_prefetch=1,
        grid=(db,),
        in_specs=[
            per_seq(rows, q2.shape[-1]),
            per_seq(rows, 1),
            pl.BlockSpec(kmask.shape, lambda b, pt: (0, 0)),
            per_seq(n_pad, NSA_KV_W),
            per_seq(n_win, NSA_KV_W),
            per_seq(n_pad, NSA_KV_W),
            pl.BlockSpec((rows, 1), lambda b, pt: (0, 0)),
            per_seq(rows, NSA_KV_HALF),
            per_seq(rows, 3),
        ] + _page_specs(n_pages, NSA_KV_W),
        out_specs=per_seq(rows, NSA_HEAD_DIM),
        scratch_shapes=[pltpu.VMEM((rows, past), F32)],
    )
    return pl.pallas_call(
        functools.partial(_nsa_sel_sample_kernel, n_pages=n_pages, n_new=n_new, past=past),
        grid_spec=grid_spec,
        out_shape=jax.ShapeDtypeStruct((db, rows, NSA_HEAD_DIM), F32),
        compiler_params=_params("arbitrary"),
        name="nsa_sel_sample",
    )(page_table, q2, selnew, kmask, news, win, neww, slope_rows, o_c, gates, *([cache] * n_pages))


def even_project(proj, pos, g_q, w_q_up, g_kv, w_kv_up):
    sizes = [MLA_Q_LORA, MLA_KV_LORA, MLA_ROPE, NSA_HEADS * NSA_HEAD_DIM, NSA_KV_W, NSA_KV_W, NSA_KV_W, 3 * NSA_HEADS]
    cuts = [int(c) for c in np.cumsum(sizes)[:-1]]
    cq, ckv, kr, nq, kvc, kvs, kvw, gt = jnp.split(proj, cuts, axis=-1)
    q = jnp.einsum('mc,chd->mhd', rmsnorm(cq, g_q), w_q_up)
    q_abs = jnp.einsum('mhn,chn->mhc', q[..., :MLA_NOPE], w_kv_up[..., :MLA_NOPE]) * MLA_SCALE
    q_rope = rope_rows(q[..., MLA_NOPE:], pos) * MLA_SCALE
    q_cat = jnp.concatenate([q_abs, q_rope], axis=-1).astype(BF16)
    ckv_rows = jnp.concatenate([rmsnorm(ckv, g_kv), rope_rows(kr, pos)], axis=-1)
    nq = (nq.reshape(-1, NSA_KV_HEADS, NSA_GROUP, NSA_HEAD_DIM) * NSA_SCALE).astype(BF16)
    gates = jax.nn.sigmoid(gt.reshape(-1, NSA_KV_HEADS, NSA_GROUP, 3))
    return q_cat, ckv_rows, nq, kvc, kvs, kvw, gates


def chunk_partials(rows, w):
    B, L = rows.shape[:2]
    rows = pad_rows(rows, L + (-L) % CMP_STRIDE)
    ch = rows.reshape((B, rows.shape[1] // CMP_STRIDE, CMP_STRIDE) + rows.shape[2:])
    first = jnp.einsum('bcjegd,ejg->bcegd', ch, w[:, :CMP_STRIDE], precision=lax.Precision.HIGHEST)
    second = jnp.einsum('bcjegd,ejg->bcegd', ch, w[:, CMP_STRIDE:], precision=lax.Precision.HIGHEST)
    return first, second


def _cmp_weight_rows(w_cmp):
    w = jnp.transpose(w_cmp, (1, 0, 2))
    w = jnp.broadcast_to(w[..., None], w.shape + (NSA_HEAD_DIM,)).reshape(CMP_BLOCK, NSA_KV_W)
    return w.reshape(2, CMP_STRIDE, NSA_KV_W)


def even_mixer_prompt(q_cat, ckv_rows, nq, kvc, kvs, kvw, gates, w_up, w_cmp, B, T):
    slopes = alibi_slopes()
    o_m = mla_prompt(jnp.transpose(q_cat.reshape(B, T, MLA_HEADS, MLA_ROW), (0, 2, 1, 3)),
                     ckv_rows.reshape(B, T, MLA_ROW).astype(BF16), w_up)
    q = jnp.transpose(nq.reshape(B, T, NSA_KV_HEADS, NSA_GROUP, NSA_HEAD_DIM), (0, 2, 3, 1, 4))
    kv5 = (B, T, 2, NSA_KV_HEADS, NSA_HEAD_DIM)
    first, second = chunk_partials(kvc.reshape(kv5), w_cmp)
    kc = first[:, :-1] + second[:, 1:]
    n_cmp = kc.shape[1]
    n_pad = -(-n_cmp // 128) * 128
    kc = jnp.transpose(pad_rows(kc, n_pad), (2, 0, 3, 1, 4)).astype(BF16)
    o_c, sel = nsa_cmp_prompt(q, kc[0], kc[1], slopes, n_cmp)
    nsb = sel.shape[-1]
    q_sel = jnp.concatenate([q, jnp.broadcast_to((1.0 - sel).astype(BF16)[:, :, None], q.shape[:4] + (nsb,))], axis=-1)
    ks = jnp.transpose(kvs.reshape(kv5), (2, 0, 3, 1, 4)).astype(BF16)
    own_blk = (jnp.arange(T)[:, None] // SEL_BLOCK == jnp.arange(nsb)[None, :])
    k_mask = jnp.where(own_blk, NEG_BIG, 0.0).astype(BF16)
    k_sel = jnp.concatenate([ks[0], jnp.broadcast_to(k_mask, ks[0].shape[:3] + (nsb,))], axis=-1)
    o_s = nsa_flash_prompt(q_sel, k_sel, ks[1], slopes, n_band=0)
    kw = jnp.transpose(kvw.reshape(kv5), (2, 0, 3, 1, 4)).astype(BF16)
    o_w = nsa_flash_prompt(q, kw[0], kw[1], slopes, n_band=WINDOW // ATT_TILE + 1)
    gt = gates.reshape(B, T, NSA_HEADS, 3)

    def gated(o, k):
        return o.reshape(B, T, NSA_HEADS, NSA_HEAD_DIM) * gt[..., k:k + 1]

    o_n = (gated(o_c, 0) + gated(o_s, 1) + gated(o_w, 2)).reshape(B, T, NSA_HEADS * NSA_HEAD_DIM)
    return jnp.concatenate([o_m, o_n], axis=-1)


def even_mixer_sample(q_cat, ckv_rows, nq, kvc, kvs, kvw, gates, w_up, w_cmp, page_table,
                      c_mla, c_cmp, c_sel, c_win, DB, DS):
    n_pages = page_table.shape[1]
    past = n_pages * PAGE_SIZE
    slopes = alibi_slopes()
    n_pad = CMP_STRIDE

    def pad_new(a):
        return pad_rows(a.reshape(DB, DS, -1), n_pad)

    q_m = jnp.transpose(q_cat.reshape(DB, DS, MLA_HEADS, MLA_ROW), (0, 2, 1, 3)).reshape(DB, MLA_HEADS * DS, MLA_ROW)
    o_m = mla_sample(page_table, q_m, pad_new(ckv_rows).astype(BF16), w_up, c_mla)
    rows = NSA_KV_HEADS * DS * NSA_GROUP
    q = jnp.transpose(nq.reshape(DB, DS, NSA_KV_HEADS, NSA_GROUP, NSA_HEAD_DIM), (0, 2, 1, 3, 4))
    eye = jnp.eye(NSA_KV_HEADS, dtype=BF16)
    q_bd = (q[:, :, :, :, None, :] * eye[None, :, None, None, :, None]).reshape(DB, rows, NSA_KV_HALF)
    slope_rows = jnp.broadcast_to(slopes[:, None, :], (NSA_KV_HEADS, DS, NSA_GROUP)).reshape(rows, 1)
    o_c, sel = nsa_cmp_sample(page_table, q_bd, pad_new(kvc), _cmp_weight_rows(w_cmp), slope_rows,
                              c_cmp.reshape(c_cmp.shape[0], PAGE_SIZE, NSA_KV_W), DS)
    npb = past // SEL_BLOCK
    unsel = jnp.repeat((1.0 - sel[..., :npb]).astype(BF16), NSA_GROUP, axis=1)
    q2 = jnp.concatenate([q_bd, unsel], axis=-1)
    selnew = jnp.repeat(sel[..., npb:npb + 1], NSA_GROUP, axis=1)
    own_blk = (jnp.arange(past)[:, None] // SEL_BLOCK == jnp.arange(npb)[None, :])
    kmask = jnp.where(own_blk, NEG_BIG, 0.0).astype(BF16)
    g_rows = jnp.transpose(gates.reshape(DB, DS, NSA_KV_HEADS, NSA_GROUP, 3), (0, 2, 1, 3, 4)).reshape(DB, rows, 3)
    o_n = nsa_sel_sample(page_table, q2, selnew, kmask, pad_new(kvs), c_win.reshape(DB, -1, NSA_KV_W),
                         pad_new(kvw), slope_rows, o_c, g_rows,
                         c_sel.reshape(c_sel.shape[0], PAGE_SIZE, NSA_KV_W), DS)
    o_n = jnp.transpose(o_n.reshape(DB, NSA_KV_HEADS, DS, NSA_GROUP, NSA_HEAD_DIM), (0, 2, 1, 3, 4))
    return jnp.concatenate([o_m, o_n.reshape(DB, DS, NSA_HEADS * NSA_HEAD_DIM)], axis=-1)


def gated_delta(q, k, v, g, beta, S0):
    B, T, H, _ = q.shape
    DV = v.shape[-1]
    C = min(DN_CHUNK, T)
    Tp = T + (-T) % C
    n = Tp // C

    def prep(x):
        x = pad_rows(x.astype(F32), Tp)
        return jnp.moveaxis(x.reshape((B, n, C) + x.shape[2:]), 3, 1)

    q, k, v, g, beta = prep(q), prep(k), prep(v), prep(g), prep(beta)
    gc = jnp.cumsum(g, axis=-1)
    i = jnp.arange(C)
    causal = i[:, None] >= i[None, :]
    strict = i[:, None] > i[None, :]
    decay = jnp.exp(jnp.where(causal, gc[..., :, None] - gc[..., None, :], -jnp.inf))
    kb = k * beta[..., None]
    lower = jnp.einsum('bhnid,bhnjd->bhnij', kb, k) * decay * strict
    a_mat = lower + jnp.eye(C, dtype=F32)
    rhs = jnp.concatenate([v * beta[..., None], kb * jnp.exp(gc)[..., None]], axis=-1)
    sol = lax.linalg.triangular_solve(a_mat, rhs, left_side=True, lower=True, unit_diagonal=True)
    u, w = sol[..., :DV], sol[..., DV:]
    attn = jnp.einsum('bhnid,bhnjd->bhnij', q, k) * decay
    qd = q * jnp.exp(gc)[..., None]
    kd = k * jnp.exp(gc[..., -1:] - gc)[..., None]
    gl = jnp.exp(gc[..., -1])

    def step(S, xs):
        u_c, w_c, qd_c, kd_c, at_c, gl_c = xs
        v_new = u_c - jnp.einsum('bhcd,bhde->bhce', w_c, S)
        o = jnp.einsum('bhcd,bhde->bhce', qd_c, S) + jnp.einsum('bhij,bhje->bhie', at_c, v_new)
        S = S * gl_c[..., None, None] + jnp.einsum('bhcd,bhce->bhde', kd_c, v_new)
        return S, o

    xs = tuple(jnp.moveaxis(a, 2, 0) for a in (u, w, qd, kd, attn, gl))
    S, o = lax.scan(step, S0.astype(F32), xs)
    o = jnp.moveaxis(o, 0, 2).reshape(B, H, Tp, DV)[:, :, :T]
    return jnp.transpose(o, (0, 2, 1, 3)), S


def delta_mixer(proj, conv_buf, S0, w_conv, a_log, dt_bias, g_norm):
    B, T, _ = proj.shape
    vd = DN_V_HEADS * DN_V_DIM
    qkv = proj[..., :DN_CONV_DIM]
    z = proj[..., DN_CONV_DIM:DN_CONV_DIM + vd]
    b = proj[..., DN_CONV_DIM + vd:DN_CONV_DIM + vd + DN_V_HEADS]
    a = proj[..., DN_CONV_DIM + vd + DN_V_HEADS:]
    xp = jnp.concatenate([conv_buf.astype(qkv.dtype), qkv], axis=1)
    conv = xp[:, 0:T] * w_conv[0]
    for j in range(1, DN_CONV):
        conv = conv + xp[:, j:j + T] * w_conv[j]
    conv = jax.nn.silu(conv)
    qd = DN_QK_HEADS * DN_K_DIM
    rep = DN_V_HEADS // DN_QK_HEADS
    q = jnp.repeat(l2norm(conv[..., :qd].reshape(B, T, DN_QK_HEADS, DN_K_DIM)), rep, axis=2) * DN_K_DIM ** -0.5
    k = jnp.repeat(l2norm(conv[..., qd:2 * qd].reshape(B, T, DN_QK_HEADS, DN_K_DIM)), rep, axis=2)
    v = conv[..., 2 * qd:].reshape(B, T, DN_V_HEADS, DN_V_DIM)
    beta = jax.nn.sigmoid(b.astype(F32))
    g = -jnp.exp(a_log.astype(F32)) * jax.nn.softplus(a.astype(F32) + dt_bias.astype(F32))
    o, S = gated_delta(q, k, v, g, beta, S0)
    o = rmsnorm(o, g_norm) * jax.nn.silu(z.reshape(B, T, DN_V_HEADS, DN_V_DIM).astype(F32))
    return o.reshape(B, T, vd), xp[:, T:], S


def kernel(x_prompt, x_sample, cache_mla, cache_nsa_cmp, cache_nsa_sel, cache_nsa_win, state_delta,
           state_delta_conv, page_table, norm_g, w_ffn_in, w_ffn_out, w_in_even, g_q_lora, w_q_up,
           g_kv_lora, w_kv_up, w_cmp, w_o_even, w_in_odd, w_conv, a_log, dt_bias, g_dn_norm, w_out_odd):
    B, T, D = x_prompt.shape
    DB, DS, _ = x_sample.shape
    n_p = B * T
    n_s = DB * DS
    past = page_table.shape[1] * PAGE_SIZE
    x = jnp.concatenate([x_prompt.reshape(n_p, D), x_sample.reshape(n_s, D)], axis=0)
    pos = jnp.concatenate([jnp.tile(jnp.arange(T), B), jnp.tile(past + jnp.arange(DS), DB)])
    kv5 = (2, NSA_KV_HEADS, NSA_HEAD_DIM)

    mla_p, mla_s, cmp_p, cmp_s, sel_p, sel_s, win_p, win_s = [], [], [], [], [], [], [], []
    dst_p, dst_s, dcv_p, dcv_s = [], [], [], []
    for l in range(DEPTH):
        ng = norm_g[l]
        i = l // 2
        x = ffn_half(x, ng[0], ng[1], w_ffn_in[l, 0].astype(BF16), w_ffn_out[l, 0].astype(BF16))
        if l % 2 == 0:
            n_in = w_in_even.shape[-1]
            proj = norm_proj(x, ng[2], w_in_even[i].astype(BF16), tm=256, tn=n_in)
            q_cat, ckv_rows, nq, kvc, kvs, kvw, gates = even_project(
                proj, pos, g_q_lora[i], w_q_up[i], g_kv_lora[i], w_kv_up[i])
            w_up = jnp.transpose(w_kv_up[i][..., MLA_NOPE:], (1, 0, 2)).astype(BF16)
            op = even_mixer_prompt(q_cat[:n_p], ckv_rows[:n_p], nq[:n_p], kvc[:n_p], kvs[:n_p], kvw[:n_p],
                                   gates[:n_p], w_up, w_cmp[i], B, T)
            os_ = even_mixer_sample(q_cat[n_p:], ckv_rows[n_p:], nq[n_p:], kvc[n_p:], kvs[n_p:], kvw[n_p:],
                                    gates[n_p:], w_up, w_cmp[i], page_table, cache_mla[i], cache_nsa_cmp[i],
                                    cache_nsa_sel[i], cache_nsa_win[i], DB, DS)
            mla_p.append(ckv_rows[:n_p].reshape(B, T, MLA_ROW))
            mla_s.append(ckv_rows[n_p:].reshape(DB, DS, MLA_ROW))
            cmp_p.append(kvc[:n_p].reshape((B, T) + kv5)); cmp_s.append(kvc[n_p:].reshape((DB, DS) + kv5))
            sel_p.append(kvs[:n_p].reshape((B, T) + kv5)); sel_s.append(kvs[n_p:].reshape((DB, DS) + kv5))
            kvw_p = kvw[:n_p].reshape((B, T) + kv5)
            win_p.append(kvw_p[:, T - min(WINDOW, T):])
            win_s.append(jnp.concatenate([cache_nsa_win[i], kvw[n_p:].reshape((DB, DS) + kv5)], axis=1)[:, DS:])
            w_o = w_o_even[i]
        else:
            n_main = DN_CONV_DIM + DN_V_HEADS * DN_V_DIM
            w_in = w_in_odd[i]
            proj_main = norm_proj(x, ng[2], w_in[:, :n_main].astype(BF16), tm=512, tn=1024)
            proj_ba = norm_proj(x, ng[2], w_in[:, n_main:].astype(BF16), tm=512, tn=2 * DN_V_HEADS)
            proj = jnp.concatenate([proj_main, proj_ba], axis=-1)
            n_in = proj.shape[-1]
            dw = (w_conv[i], a_log[i], dt_bias[i], g_dn_norm[i])
            buf0 = jnp.zeros((B, DN_CONV - 1, DN_CONV_DIM), F32)
            S0 = jnp.zeros((B, DN_V_HEADS, DN_K_DIM, DN_V_DIM), F32)
            op, c_p, S_p = delta_mixer(proj[:n_p].reshape(B, T, n_in), buf0, S0, *dw)
            os_, c_s, S_s = delta_mixer(proj[n_p:].reshape(DB, DS, n_in), state_delta_conv[i], state_delta[i], *dw)
            dst_p.append(S_p.astype(state_delta.dtype)); dst_s.append(S_s.astype(state_delta.dtype))
            dcv_p.append(c_p); dcv_s.append(c_s)
            w_o = w_out_odd[i]
        o = jnp.concatenate([op.reshape(n_p, -1), os_.reshape(n_s, -1)], axis=0)
        x = out_proj_residual(o, w_o.astype(BF16), x, ng[3])
        x = ffn_half(x, ng[4], ng[5], w_ffn_in[l, 1].astype(BF16), w_ffn_out[l, 1].astype(BF16))
    return (x[:n_p].reshape(B, T, D), x[n_p:].reshape(DB, DS, D),
            jnp.stack(mla_p), jnp.stack(mla_s),
            jnp.stack(cmp_p), jnp.stack(cmp_s),
            jnp.stack(sel_p), jnp.stack(sel_s),
            jnp.stack(win_p), jnp.stack(win_s),
            jnp.stack(dst_p), jnp.stack(dst_s),
            jnp.stack(dcv_p), jnp.stack(dcv_s))
```

```python
import functools

import jax
import jax.numpy as jnp
import numpy as np
from jax import lax
from jax.experimental import pallas as pl
from jax.experimental.pallas import tpu as pltpu

D_MODEL = 2048
DEPTH = 4
PAGE_SIZE = 128
MLA_HEADS = 8
MLA_Q_LORA = 512
MLA_KV_LORA = 256
MLA_NOPE = 128
MLA_ROPE = 64
MLA_V = 128
MLA_ROW = MLA_KV_LORA + MLA_ROPE
ROPE_THETA = 10000.0
MLA_SCALE = (MLA_NOPE + MLA_ROPE) ** -0.5
NSA_HEADS = 16
NSA_KV_HEADS = 2
NSA_GROUP = NSA_HEADS // NSA_KV_HEADS
NSA_HEAD_DIM = 64
NSA_SCALE = NSA_HEAD_DIM ** -0.5
CMP_STRIDE = 16
CMP_BLOCK = 2 * CMP_STRIDE
SEL_BLOCK = 64
SEL_TOP = 16
SEL_RATIO = SEL_BLOCK // CMP_STRIDE
WINDOW = 512
FORCE_BONUS = 1.0e4
DN_QK_HEADS = 16
DN_V_HEADS = 32
DN_K_DIM = 128
DN_V_DIM = 128
DN_CONV = 4
DN_CHUNK = 64
DN_CONV_DIM = 2 * DN_QK_HEADS * DN_K_DIM + DN_V_HEADS * DN_V_DIM
D_FF = 5632
NORM_EPS = 1e-6
L2_EPS = 1e-6
NEG_BIG = -1e30
MASK_FILL = 2.0 * NEG_BIG
TINY = 1e-30
NSA_KV_W = 2 * NSA_KV_HEADS * NSA_HEAD_DIM
NSA_KV_HALF = NSA_KV_HEADS * NSA_HEAD_DIM

F32 = jnp.float32
BF16 = jnp.bfloat16

VMEM_LIMIT_BYTES = 56 * 1024 * 1024
ATT_TILE = 256

_NT = (((1,), (1,)), ((), ()))


def _params(*sem):
    return pltpu.CompilerParams(dimension_semantics=sem, vmem_limit_bytes=VMEM_LIMIT_BYTES)


def _ffn_kernel(x_ref, gpre_ref, gpost_ref, wg_ref, wu_ref, wo_ref, o_ref, h_scr, acc_scr):
    j = pl.program_id(1)

    @pl.when(j == 0)
    def _():
        x = x_ref[...]
        ms = jnp.mean(x * x, axis=-1, keepdims=True)
        h_scr[...] = (x * lax.rsqrt(ms + NORM_EPS) * gpre_ref[...]).astype(BF16)
        acc_scr[...] = jnp.zeros_like(acc_scr)

    h = h_scr[...]
    gate = jnp.dot(h, wg_ref[...], preferred_element_type=F32)
    up = jnp.dot(h, wu_ref[...], preferred_element_type=F32)
    a = (gate * jax.nn.sigmoid(gate) * up).astype(BF16)
    acc_scr[...] += jnp.dot(a, wo_ref[...], preferred_element_type=F32)

    @pl.when(j == pl.num_programs(1) - 1)
    def _():
        y = acc_scr[...]
        ms = jnp.mean(y * y, axis=-1, keepdims=True)
        o_ref[...] = x_ref[...] + 0.5 * (y * lax.rsqrt(ms + NORM_EPS) * gpost_ref[...])


def ffn_half(x, g_pre, g_post, w_in, w_out, tm=512, tf=512):
    m, d = x.shape
    f = w_out.shape[0]
    nf = f // tf
    assert m % tm == 0 and f % tf == 0
    return pl.pallas_call(
        _ffn_kernel,
        grid=(m // tm, nf),
        in_specs=[
            pl.BlockSpec((tm, d), lambda i, j: (i, 0)),
            pl.BlockSpec((1, d), lambda i, j: (0, 0)),
            pl.BlockSpec((1, d), lambda i, j: (0, 0)),
            pl.BlockSpec((d, tf), lambda i, j: (0, j)),
            pl.BlockSpec((d, tf), lambda i, j: (0, j + nf)),
            pl.BlockSpec((tf, d), lambda i, j: (j, 0)),
        ],
        out_specs=pl.BlockSpec((tm, d), lambda i, j: (i, 0)),
        out_shape=jax.ShapeDtypeStruct((m, d), F32),
        scratch_shapes=[pltpu.VMEM((tm, d), BF16), pltpu.VMEM((tm, d), F32)],
        compiler_params=_params("arbitrary", "arbitrary"),
        name="ffn_half",
    )(x, g_pre.reshape(1, d), g_post.reshape(1, d), w_in, w_in, w_out)


def _norm_proj_kernel(x_ref, g_ref, w_ref, o_ref, h_scr):
    @pl.when(pl.program_id(1) == 0)
    def _():
        x = x_ref[...]
        ms = jnp.mean(x * x, axis=-1, keepdims=True)
        h_scr[...] = (x * lax.rsqrt(ms + NORM_EPS) * g_ref[...]).astype(BF16)

    o_ref[...] = jnp.dot(h_scr[...], w_ref[...], preferred_element_type=F32)


def norm_proj(x, g, w, tm, tn):
    m, d = x.shape
    n = w.shape[1]
    assert m % tm == 0 and n % tn == 0
    return pl.pallas_call(
        _norm_proj_kernel,
        grid=(m // tm, n // tn),
        in_specs=[
            pl.BlockSpec((tm, d), lambda i, j: (i, 0)),
            pl.BlockSpec((1, d), lambda i, j: (0, 0)),
            pl.BlockSpec((d, tn), lambda i, j: (0, j)),
        ],
        out_specs=pl.BlockSpec((tm, tn), lambda i, j: (i, j)),
        out_shape=jax.ShapeDtypeStruct((m, n), F32),
        scratch_shapes=[pltpu.VMEM((tm, d), BF16)],
        compiler_params=_params("arbitrary", "arbitrary"),
        name="norm_proj",
    )(x, g.reshape(1, d), w)


def _out_proj_kernel(a_ref, w_ref, x_ref, g_ref, o_ref):
    y = jnp.dot(a_ref[...].astype(BF16), w_ref[...], preferred_element_type=F32)
    ms = jnp.mean(y * y, axis=-1, keepdims=True)
    o_ref[...] = x_ref[...] + y * lax.rsqrt(ms + NORM_EPS) * g_ref[...]


def out_proj_residual(a, w, x, g, tm=256):
    m, k = a.shape
    d = w.shape[1]
    assert m % tm == 0
    return pl.pallas_call(
        _out_proj_kernel,
        grid=(m // tm,),
        in_specs=[
            pl.BlockSpec((tm, k), lambda i: (i, 0)),
            pl.BlockSpec((k, d), lambda i: (0, 0)),
            pl.BlockSpec((tm, d), lambda i: (i, 0)),
            pl.BlockSpec((1, d), lambda i: (0, 0)),
        ],
        out_specs=pl.BlockSpec((tm, d), lambda i: (i, 0)),
        out_shape=jax.ShapeDtypeStruct((m, d), F32),
        compiler_params=_params("arbitrary"),
        name="out_proj_residual",
    )(a, w, x, g.reshape(1, d))


def rmsnorm(x, g):
    xf = x.astype(F32)
    y = xf * lax.rsqrt(jnp.mean(xf * xf, axis=-1, keepdims=True) + NORM_EPS)
    return (y * g.astype(F32)).astype(x.dtype)


def l2norm(x):
    xf = x.astype(F32)
    return xf * lax.rsqrt(jnp.sum(xf * xf, axis=-1, keepdims=True) + L2_EPS)


def rope_rows(x, pos):
    half = x.shape[-1] // 2
    inv = ROPE_THETA ** (-2.0 * jnp.arange(half, dtype=F32) / x.shape[-1])
    ang = pos.astype(F32)[:, None] * inv
    shape = (pos.shape[0],) + (1,) * (x.ndim - 2) + (half,)
    cos = jnp.cos(ang).reshape(shape)
    sin = jnp.sin(ang).reshape(shape)
    x1 = x[..., :half]
    x2 = x[..., half:]
    return jnp.concatenate([x1 * cos - x2 * sin, x1 * sin + x2 * cos], axis=-1)


def alibi_slopes():
    h = jnp.arange(NSA_HEADS, dtype=F32) + 1.0
    return (2.0 ** (-8.0 * h / NSA_HEADS)).reshape(NSA_KV_HEADS, NSA_GROUP)


def pad_rows(x, length):
    return jnp.pad(x, [(0, 0), (0, length - x.shape[1])] + [(0, 0)] * (x.ndim - 2))


def _masked_softmax_parts(s, valid):
    s = jnp.where(valid, s, NEG_BIG)
    m = jnp.max(s, axis=-1, keepdims=True)
    e = jnp.where(valid, jnp.exp(s - m), 0.0)
    return e, jnp.sum(e, axis=-1, keepdims=True)


def _topk_mask(score, k):
    n = score.shape[-1]
    lane = lax.broadcasted_iota(jnp.int32, score.shape, score.ndim - 1)
    sel = jnp.zeros(score.shape, F32)
    x = score
    for _ in range(k):
        m = jnp.max(x, axis=-1, keepdims=True)
        idx = jnp.min(jnp.where(x == m, lane, n), axis=-1, keepdims=True)
        hit = lane == idx
        sel = jnp.where(hit & (m > -jnp.inf), 1.0, sel)
        x = jnp.where(hit, -jnp.inf, x)
    return sel


def _block_scores(blk_imp, q_pos):
    blk = lax.broadcasted_iota(jnp.int32, blk_imp.shape, 1)
    cur = q_pos // SEL_BLOCK
    valid = blk <= cur
    forced = valid & ((blk == 0) | (blk >= cur - 1))
    return jnp.where(valid, blk_imp + FORCE_BONUS * forced.astype(F32), -jnp.inf)


def _imp_to_block_matrix(n_cmp, n_blk):
    c = np.arange(n_cmp)[:, None]
    n = np.arange(n_blk)[None, :]
    return jnp.asarray(((c // SEL_RATIO == n) | (c == SEL_RATIO * n - 1)).astype(np.float32))


def _mla_prompt_kernel(q_ref, k_ref, wup_ref, o_ref, m_scr, l_scr, acc_scr):
    i = pl.program_id(1)
    j = pl.program_id(2)
    heads, tq, dq = q_ref.shape
    tk = k_ref.shape[0]

    @pl.when(j == 0)
    def _():
        m_scr[...] = jnp.full_like(m_scr, NEG_BIG)
        l_scr[...] = jnp.zeros_like(l_scr)
        acc_scr[...] = jnp.zeros_like(acc_scr)

    def step(diagonal):
        q = q_ref[...].reshape(heads * tq, dq)
        k = k_ref[...]
        s = lax.dot_general(q, k, _NT, preferred_element_type=F32)
        if diagonal:
            row = lax.broadcasted_iota(jnp.int32, (heads * tq, 1), 0)
            t_pos = i * tq + (row & (tq - 1))
            k_pos = j * tk + lax.broadcasted_iota(jnp.int32, (1, tk), 1)
            s = jnp.where(k_pos <= t_pos, s, MASK_FILL)
        m_old = m_scr[...]
        m_new = jnp.maximum(m_old, jnp.max(s, axis=-1, keepdims=True))
        e = jnp.exp(s - m_new)
        alpha = jnp.exp(m_old - m_new)
        l_scr[...] = alpha * l_scr[...] + jnp.sum(e, axis=-1, keepdims=True)
        acc_scr[...] = alpha * acc_scr[...] + jnp.dot(e.astype(BF16), k[:, :MLA_KV_LORA], preferred_element_type=F32)
        m_scr[...] = m_new

    @pl.when(j < i)
    def _():
        step(False)

    @pl.when(j == i)
    def _():
        step(True)
        o_lat = (acc_scr[...] / jnp.maximum(l_scr[...], TINY)).astype(BF16)
        outs = [jnp.dot(o_lat[h * tq:(h + 1) * tq], wup_ref[h], preferred_element_type=F32) for h in range(heads)]
        o_ref[...] = jnp.concatenate(outs, axis=-1)


def mla_prompt(q_cat, ckv, w_up, tile=ATT_TILE):
    b, h, t, dq = q_cat.shape
    assert t % tile == 0 and tile & (tile - 1) == 0
    nt = t // tile
    return pl.pallas_call(
        _mla_prompt_kernel,
        grid=(b, nt, nt),
        in_specs=[
            pl.BlockSpec((None, h, tile, dq), lambda b_, i, j: (b_, 0, i, 0)),
            pl.BlockSpec((None, tile, dq), lambda b_, i, j: (b_, jnp.minimum(i, j), 0)),
            pl.BlockSpec((h, MLA_KV_LORA, MLA_V), lambda b_, i, j: (0, 0, 0)),
        ],
        out_specs=pl.BlockSpec((None, tile, h * MLA_V), lambda b_, i, j: (b_, i, 0)),
        out_shape=jax.ShapeDtypeStruct((b, t, h * MLA_V), F32),
        scratch_shapes=[pltpu.VMEM((h * tile, 1), F32), pltpu.VMEM((h * tile, 1), F32),
                        pltpu.VMEM((h * tile, MLA_KV_LORA), F32)],
        compiler_params=_params("arbitrary", "arbitrary", "arbitrary"),
        name="mla_prompt",
    )(q_cat, ckv, w_up)


def _page_specs(n_pages, width):
    return [pl.BlockSpec((None, PAGE_SIZE, width), lambda b, pt, p=p: (pt[b, p], 0, 0)) for p in range(n_pages)]


def _mla_sample_kernel(pt_ref, q_ref, new_ref, wup_ref, *rest, n_pages, n_new):
    pages = rest[:n_pages]
    o_ref = rest[n_pages]
    s_scr = rest[n_pages + 1]
    q = q_ref[...]
    rows = q.shape[0]
    heads = rows // n_new
    for p in range(n_pages):
        kp = pages[p][...].astype(BF16)
        s_scr[:, p * PAGE_SIZE:(p + 1) * PAGE_SIZE] = lax.dot_general(q, kp, _NT, preferred_element_type=F32)
    new = new_ref[...]
    s_new = lax.dot_general(q, new, _NT, preferred_element_type=F32)
    t_row = lax.broadcasted_iota(jnp.int32, (rows, 1), 0) & (n_new - 1)
    valid_new = lax.broadcasted_iota(jnp.int32, (1, new.shape[0]), 1) <= t_row
    s_new = jnp.where(valid_new, s_new, NEG_BIG)
    s = s_scr[...]
    m = jnp.maximum(jnp.max(s, axis=-1, keepdims=True), jnp.max(s_new, axis=-1, keepdims=True))
    e = jnp.exp(s - m).astype(BF16)
    e_new = jnp.where(valid_new, jnp.exp(s_new - m), 0.0)
    l = jnp.sum(e.astype(F32), axis=-1, keepdims=True) + jnp.sum(e_new, axis=-1, keepdims=True)
    acc = jnp.dot(e_new.astype(BF16), new[:, :MLA_KV_LORA], preferred_element_type=F32)
    for p in range(n_pages):
        vp = pages[p][:, :MLA_KV_LORA].astype(BF16)
        acc = acc + jnp.dot(e[:, p * PAGE_SIZE:(p + 1) * PAGE_SIZE], vp, preferred_element_type=F32)
    o_lat = (acc / jnp.maximum(l, TINY)).astype(BF16)
    outs = [jnp.dot(o_lat[h * n_new:(h + 1) * n_new], wup_ref[h], preferred_element_type=F32) for h in range(heads)]
    o_ref[...] = jnp.concatenate(outs, axis=-1)


def mla_sample(page_table, q_cat, new_rows, w_up, cache):
    db, rows, dq = q_cat.shape
    n_pages = page_table.shape[1]
    heads = w_up.shape[0]
    n_new = rows // heads
    assert n_new & (n_new - 1) == 0 and n_new <= new_rows.shape[1]
    n_pad = new_rows.shape[1]
    grid_spec = pltpu.PrefetchScalarGridSpec(
        num_scalar_prefetch=1,
        grid=(db,),
        in_specs=[
            pl.BlockSpec((None, rows, dq), lambda b, pt: (b, 0, 0)),
            pl.BlockSpec((None, n_pad, dq), lambda b, pt: (b, 0, 0)),
            pl.BlockSpec((heads, MLA_KV_LORA, MLA_V), lambda b, pt: (0, 0, 0)),
        ] + _page_specs(n_pages, dq),
        out_specs=pl.BlockSpec((None, n_new, heads * MLA_V), lambda b, pt: (b, 0, 0)),
        scratch_shapes=[pltpu.VMEM((rows, n_pages * PAGE_SIZE), F32)],
    )
    return pl.pallas_call(
        functools.partial(_mla_sample_kernel, n_pages=n_pages, n_new=n_new),
        grid_spec=grid_spec,
        out_shape=jax.ShapeDtypeStruct((db, n_new, heads * MLA_V), F32),
        compiler_params=_params("arbitrary"),
        name="mla_sample",
    )(page_table, q_cat, new_rows, w_up, *([cache] * n_pages))


def _nsa_cmp_prompt_kernel(q_ref, kk_ref, kv_ref, slope_ref, a_ref, o_ref, sel_ref, *, n_cmp):
    i = pl.program_id(2)
    n_heads, tq, _ = q_ref.shape
    kk = kk_ref[...]
    kv = kv_ref[...]
    n_pad = kk.shape[0]
    t_pos = i * tq + lax.broadcasted_iota(jnp.int32, (tq, 1), 0)
    c_idx = lax.broadcasted_iota(jnp.int32, (1, n_pad), 1)
    dc = t_pos - (c_idx * CMP_STRIDE + (CMP_BLOCK - 1))
    valid = (dc >= 0) & (c_idx < n_cmp)
    dcf = dc.astype(F32)
    imp = jnp.zeros((tq, n_pad), F32)
    outs = []
    for r in range(n_heads):
        s = lax.dot_general(q_ref[r], kk, _NT, preferred_element_type=F32) - slope_ref[r] * dcf
        e, l = _masked_softmax_parts(s, valid)
        p = e / jnp.maximum(l, TINY)
        imp = imp + p
        outs.append(jnp.dot(p.astype(BF16), kv, preferred_element_type=F32))
    o_ref[...] = jnp.concatenate(outs, axis=-1)
    blk_imp = jnp.dot(imp, a_ref[...], preferred_element_type=F32, precision=lax.Precision.HIGHEST)
    sel_ref[...] = _topk_mask(_block_scores(blk_imp, t_pos), min(SEL_TOP, blk_imp.shape[-1]))


def nsa_cmp_prompt(q, kc_k, kc_v, slopes, n_cmp, tile=ATT_TILE):
    b, g, r, t, dh = q.shape
    n_pad = kc_k.shape[2]
    nsb = -(-t // SEL_BLOCK)
    a_mat = _imp_to_block_matrix(n_pad, nsb)
    return pl.pallas_call(
        functools.partial(_nsa_cmp_prompt_kernel, n_cmp=n_cmp),
        grid=(b, g, t // tile),
        in_specs=[
            pl.BlockSpec((None, None, r, tile, dh), lambda b_, g_, i: (b_, g_, 0, i, 0)),
            pl.BlockSpec((None, None, n_pad, dh), lambda b_, g_, i: (b_, g_, 0, 0)),
            pl.BlockSpec((None, None, n_pad, dh), lambda b_, g_, i: (b_, g_, 0, 0)),
            pl.BlockSpec((None, r, 1, 1), lambda b_, g_, i: (g_, 0, 0, 0)),
            pl.BlockSpec((n_pad, nsb), lambda b_, g_, i: (0, 0)),
        ],
        out_specs=[
            pl.BlockSpec((None, tile, r * dh), lambda b_, g_, i: (b_, i, g_)),
            pl.BlockSpec((None, None, tile, nsb), lambda b_, g_, i: (b_, g_, i, 0)),
        ],
        out_shape=[jax.ShapeDtypeStruct((b, t, g * r * dh), F32), jax.ShapeDtypeStruct((b, g, t, nsb), F32)],
        compiler_params=_params("arbitrary", "arbitrary", "arbitrary"),
        name="nsa_cmp_prompt",
    )(q, kc_k, kc_v, slopes.reshape(g, r, 1, 1), a_mat)


def _nsa_flash_kernel(q_ref, k_ref, v_ref, slope_ref, o_ref, m_scr, l_scr, acc_scr, *, n_band):
    i = pl.program_id(2)
    jj = pl.program_id(3)
    n_heads, tq, _ = q_ref.shape
    tk = k_ref.shape[0]
    j = jj if n_band == 0 else i - (n_band - 1) + jj
    active = (j <= i) if n_band == 0 else (j >= 0)

    @pl.when(jj == 0)
    def _():
        m_scr[...] = jnp.full_like(m_scr, NEG_BIG)
        l_scr[...] = jnp.zeros_like(l_scr)
        acc_scr[...] = jnp.zeros_like(acc_scr)

    def step(positional_mask):
        k = k_ref[...]
        v = v_ref[...]
        t_pos = i * tq + lax.broadcasted_iota(jnp.int32, (tq, 1), 0)
        k_pos = j * tk + lax.broadcasted_iota(jnp.int32, (1, tk), 1)
        d = t_pos - k_pos
        q = q_ref[...].reshape(n_heads * tq, q_ref.shape[2])
        s = lax.dot_general(q, k, _NT, preferred_element_type=F32).reshape(n_heads, tq, tk)
        s = s - slope_ref[...] * d.astype(F32)[None]
        if positional_mask:
            in_range = (d >= 0) if n_band == 0 else ((d >= 0) & (d < WINDOW))
            s = jnp.where(in_range[None], s, MASK_FILL)
        m_old = m_scr[...]
        m_new = jnp.maximum(m_old, jnp.max(s, axis=-1, keepdims=True))
        e = jnp.exp(s - m_new)
        alpha = jnp.exp(m_old - m_new)
        l_scr[...] = alpha * l_scr[...] + jnp.sum(e, axis=-1, keepdims=True)
        pv = jnp.dot(e.astype(BF16).reshape(n_heads * tq, tk), v, preferred_element_type=F32)
        acc_scr[...] = alpha * acc_scr[...] + pv.reshape(acc_scr.shape)
        m_scr[...] = m_new

    if n_band == 0:
        @pl.when(j < i)
        def _():
            step(False)

        @pl.when(j == i)
        def _():
            step(True)
    else:
        @pl.when(active)
        def _():
            step(True)

    @pl.when(jj == pl.num_programs(3) - 1)
    def _():
        outs = [acc_scr[r] / jnp.maximum(l_scr[r], TINY) for r in range(n_heads)]
        o_ref[...] = jnp.concatenate(outs, axis=-1)


def nsa_flash_prompt(q, k, v, slopes, n_band, tile=ATT_TILE):
    b, g, r, t, dq = q.shape
    dh = v.shape[-1]
    nt = t // tile
    if n_band == 0:
        n_steps = nt
        kv_map = lambda b_, g_, i, jj: (b_, g_, jnp.minimum(i, jj), 0)
    else:
        n_steps = n_band
        kv_map = lambda b_, g_, i, jj: (b_, g_, jnp.maximum(i - (n_band - 1) + jj, 0), 0)
    return pl.pallas_call(
        functools.partial(_nsa_flash_kernel, n_band=n_band),
        grid=(b, g, nt, n_steps),
        in_specs=[
            pl.BlockSpec((None, None, r, tile, dq), lambda b_, g_, i, jj: (b_, g_, 0, i, 0)),
            pl.BlockSpec((None, None, tile, dq), kv_map),
            pl.BlockSpec((None, None, tile, dh), kv_map),
            pl.BlockSpec((None, r, 1, 1), lambda b_, g_, i, jj: (g_, 0, 0, 0)),
        ],
        out_specs=pl.BlockSpec((None, tile, r * dh), lambda b_, g_, i, jj: (b_, i, g_)),
        out_shape=jax.ShapeDtypeStruct((b, t, g * r * dh), F32),
        scratch_shapes=[pltpu.VMEM((r, tile, 1), F32), pltpu.VMEM((r, tile, 1), F32), pltpu.VMEM((r, tile, dh), F32)],
        compiler_params=_params("arbitrary", "arbitrary", "arbitrary", "arbitrary"),
        name="nsa_flash_sel" if n_band == 0 else "nsa_flash_win",
    )(q, k, v, slopes.reshape(g, r, 1, 1))


def _nsa_cmp_sample_kernel(pt_ref, q_ref, new_ref, w_ref, slope_ref, a_ref, *rest, n_pages, n_new, past, nsb):
    pages = rest[:n_pages]
    o_ref, sel_ref = rest[n_pages:n_pages + 2]
    f_scr, s_scr = rest[n_pages + 2:]
    cpp = PAGE_SIZE // CMP_STRIDE
    nc = n_pages * cpp
    w1 = w_ref[0]
    w2 = w_ref[1]
    for p in range(n_pages):
        x = pages[p][...].reshape(cpp, CMP_STRIDE, NSA_KV_W)
        f_scr[p * cpp:(p + 1) * cpp] = jnp.sum(x * w1[None], axis=1)
        s_scr[p * cpp:(p + 1) * cpp] = jnp.sum(x * w2[None], axis=1)
    xn = new_ref[...]
    row8 = lax.broadcasted_iota(jnp.int32, (8, 1), 0)
    f_scr[nc:nc + 8] = jnp.where(row8 == 0, jnp.sum(xn * w1, axis=0, keepdims=True), 0.0)
    s_scr[nc:nc + 8] = jnp.where(row8 == 0, jnp.sum(xn * w2, axis=0, keepdims=True), 0.0)
    kc = f_scr[0:nc] + s_scr[1:nc + 1]
    kk = kc[:, :NSA_KV_HALF].astype(BF16)
    kv = kc[:, NSA_KV_HALF:].astype(BF16)
    q = q_ref[...]
    rows = q.shape[0]
    row = lax.broadcasted_iota(jnp.int32, (rows, 1), 0)
    q_pos = past + ((row // NSA_GROUP) & (n_new - 1))
    c_idx = lax.broadcasted_iota(jnp.int32, (1, nc), 1)
    dc = q_pos - (c_idx * CMP_STRIDE + (CMP_BLOCK - 1))
    valid = dc >= 0
    s = lax.dot_general(q, kk, _NT, preferred_element_type=F32) - slope_ref[...] * dc.astype(F32)
    e, l = _masked_softmax_parts(s, valid)
    p = e / jnp.maximum(l, TINY)
    o_ref[...] = jnp.dot(p.astype(BF16), kv, preferred_element_type=F32)
    imp = jnp.sum(p.reshape(rows // NSA_GROUP, NSA_GROUP, nc), axis=1)
    blk_imp = jnp.dot(imp, a_ref[...], preferred_element_type=F32, precision=lax.Precision.HIGHEST)
    row_g = lax.broadcasted_iota(jnp.int32, (rows // NSA_GROUP, 1), 0)
    score = _block_scores(blk_imp, past + (row_g & (n_new - 1)))
    blk = lax.broadcasted_iota(jnp.int32, score.shape, 1)
    score = jnp.where(blk < nsb, score, -jnp.inf)
    sel_ref[...] = _topk_mask(score, min(SEL_TOP, nsb))


def nsa_cmp_sample(page_table, q_bd, new_rows, w12, slope_rows, cache, n_new):
    db, rows, _ = q_bd.shape
    n_pages = page_table.shape[1]
    past = n_pages * PAGE_SIZE
    nc = past // CMP_STRIDE
    nsb = -(-(past + n_new) // SEL_BLOCK)
    nsb_pad = -(-nsb // 128) * 128
    assert n_new <= CMP_STRIDE and new_rows.shape[1] == CMP_STRIDE and n_new & (n_new - 1) == 0
    a_mat = _imp_to_block_matrix(nc, nsb_pad)
    grid_spec = pltpu.PrefetchScalarGridSpec(
        num_scalar_prefetch=1,
        grid=(db,),
        in_specs=[
            pl.BlockSpec((None, rows, NSA_KV_HALF), lambda b, pt: (b, 0, 0)),
            pl.BlockSpec((None, CMP_STRIDE, NSA_KV_W), lambda b, pt: (b, 0, 0)),
            pl.BlockSpec((2, CMP_STRIDE, NSA_KV_W), lambda b, pt: (0, 0, 0)),
            pl.BlockSpec((rows, 1), lambda b, pt: (0, 0)),
            pl.BlockSpec((nc, nsb_pad), lambda b, pt: (0, 0)),
        ] + _page_specs(n_pages, NSA_KV_W),
        out_specs=[
            pl.BlockSpec((None, rows, NSA_KV_HALF), lambda b, pt: (b, 0, 0)),
            pl.BlockSpec((None, rows // NSA_GROUP, nsb_pad), lambda b, pt: (b, 0, 0)),
        ],
        scratch_shapes=[pltpu.VMEM((nc + 8, NSA_KV_W), F32), pltpu.VMEM((nc + 8, NSA_KV_W), F32)],
    )
    return pl.pallas_call(
        functools.partial(_nsa_cmp_sample_kernel, n_pages=n_pages, n_new=n_new, past=past, nsb=nsb),
        grid_spec=grid_spec,
        out_shape=[jax.ShapeDtypeStruct((db, rows, NSA_KV_HALF), F32),
                   jax.ShapeDtypeStruct((db, rows // NSA_GROUP, nsb_pad), F32)],
        compiler_params=_params("arbitrary"),
        name="nsa_cmp_sample",
    )(page_table, q_bd, new_rows, w12, slope_rows, a_mat, *([cache] * n_pages))


def _nsa_sel_sample_kernel(pt_ref, q_ref, selnew_ref, kmask_ref, news_ref, win_ref, neww_ref, slope_ref, oc_ref,
                           gate_ref, *rest, n_pages, n_new, past):
    pages = rest[:n_pages]
    o_ref = rest[n_pages]
    s_scr = rest[n_pages + 1]
    q2 = q_ref[...]
    qb = q2[:, :NSA_KV_HALF]
    rows = q2.shape[0]
    row = lax.broadcasted_iota(jnp.int32, (rows, 1), 0)
    q_pos = past + ((row // NSA_GROUP) & (n_new - 1))
    slope = slope_ref[...]
    for p in range(n_pages):
        kp = jnp.concatenate([pages[p][:, :NSA_KV_HALF].astype(BF16),
                              kmask_ref[p * PAGE_SIZE:(p + 1) * PAGE_SIZE]], axis=-1)
        s_scr[:, p * PAGE_SIZE:(p + 1) * PAGE_SIZE] = lax.dot_general(q2, kp, _NT, preferred_element_type=F32)
    k_pos = lax.broadcasted_iota(jnp.int32, (1, n_pages * PAGE_SIZE), 1)
    d = q_pos - k_pos
    s = s_scr[...] - slope * d.astype(F32)
    valid = (d >= 0) & (s > 0.1 * NEG_BIG)
    s = jnp.where(valid, s, NEG_BIG)
    news = news_ref[...]
    n_pad = news.shape[0]
    dn = q_pos - (past + lax.broadcasted_iota(jnp.int32, (1, n_pad), 1))
    valid_n = (dn >= 0) & (selnew_ref[...] > 0.5)
    s_n = (lax.dot_general(qb, news[:, :NSA_KV_HALF].astype(BF16), _NT, preferred_element_type=F32)
           - slope * dn.astype(F32))
    s_n = jnp.where(valid_n, s_n, NEG_BIG)
    m = jnp.maximum(jnp.max(s, axis=-1, keepdims=True), jnp.max(s_n, axis=-1, keepdims=True))
    e = jnp.where(valid, jnp.exp(s - m), 0.0)
    e_n = jnp.where(valid_n, jnp.exp(s_n - m), 0.0)
    l = jnp.sum(e, axis=-1, keepdims=True) + jnp.sum(e_n, axis=-1, keepdims=True)
    eb = e.astype(BF16)
    acc = jnp.dot(e_n.astype(BF16), news[:, NSA_KV_HALF:].astype(BF16), preferred_element_type=F32)
    for p in range(n_pages):
        vp = pages[p][:, NSA_KV_HALF:].astype(BF16)
        acc = acc + jnp.dot(eb[:, p * PAGE_SIZE:(p + 1) * PAGE_SIZE], vp, preferred_element_type=F32)
    o_s = acc / jnp.maximum(l, TINY)
    kw = jnp.concatenate([win_ref[...], neww_ref[...]], axis=0)
    n_win = win_ref.shape[0]
    kw_pos = past - n_win + lax.broadcasted_iota(jnp.int32, (1, kw.shape[0]), 1)
    dw = q_pos - kw_pos
    valid_w = (dw >= 0) & (dw < WINDOW) & (kw_pos >= 0)
    s_w = (lax.dot_general(qb, kw[:, :NSA_KV_HALF].astype(BF16), _NT, preferred_element_type=F32)
           - slope * dw.astype(F32))
    e_w, l_w = _masked_softmax_parts(s_w, valid_w)
    p_w = e_w / jnp.maximum(l_w, TINY)
    o_w = jnp.dot(p_w.astype(BF16), kw[:, NSA_KV_HALF:].astype(BF16), preferred_element_type=F32)
    gate = gate_ref[...]
    o = gate[:, 0:1] * oc_ref[...] + gate[:, 1:2] * o_s + gate[:, 2:3] * o_w
    o_ref[...] = jnp.where(row < rows // NSA_KV_HEADS, o[:, :NSA_HEAD_DIM], o[:, NSA_HEAD_DIM:])


def nsa_sel_sample(page_table, q2, selnew, kmask, news, win, neww, slope_rows, o_c, gates, cache, n_new):
    db, rows, _ = q2.shape
    n_pages = page_table.shape[1]
    past = n_pages * PAGE_SIZE
    n_win = win.shape[1]
    n_pad = news.shape[1]
    per_seq = lambda *shape: pl.BlockSpec((None,) + shape, lambda b, pt: (b,) + (0,) * len(shape))
    grid_spec = pltpu.PrefetchScalarGridSpec(
        num_scalar_prefetch=1,
        grid=(db,),
        in_specs=[
            per_seq(rows, q2.shape[-1]),
            per_seq(rows, 1),
            pl.BlockSpec(kmask.shape, lambda b, pt: (0, 0)),
            per_seq(n_pad, NSA_KV_W),
            per_seq(n_win, NSA_KV_W),
            per_seq(n_pad, NSA_KV_W),
            pl.BlockSpec((rows, 1), lambda b, pt: (0, 0)),
            per_seq(rows, NSA_KV_HALF),
            per_seq(rows, 3),
        ] + _page_specs(n_pages, NSA_KV_W),
        out_specs=per_seq(rows, NSA_HEAD_DIM),
        scratch_shapes=[pltpu.VMEM((rows, past), F32)],
    )
    return pl.pallas_call(
        functools.partial(_nsa_sel_sample_kernel, n_pages=n_pages, n_new=n_new, past=past),
        grid_spec=grid_spec,
        out_shape=jax.ShapeDtypeStruct((db, rows, NSA_HEAD_DIM), F32),
        compiler_params=_params("arbitrary"),
        name="nsa_sel_sample",
    )(page_table, q2, selnew, kmask, news, win, neww, slope_rows, o_c, gates, *([cache] * n_pages))


def even_project(proj, pos, g_q, w_q_up, g_kv, w_kv_up):
    sizes = [MLA_Q_LORA, MLA_KV_LORA, MLA_ROPE, NSA_HEADS * NSA_HEAD_DIM, NSA_KV_W, NSA_KV_W, NSA_KV_W, 3 * NSA_HEADS]
    cuts = [int(c) for c in np.cumsum(sizes)[:-1]]
    cq, ckv, kr, nq, kvc, kvs, kvw, gt = jnp.split(proj, cuts, axis=-1)
    q = jnp.einsum('mc,chd->mhd', rmsnorm(cq, g_q), w_q_up)
    q_abs = jnp.einsum('mhn,chn->mhc', q[..., :MLA_NOPE], w_kv_up[..., :MLA_NOPE]) * MLA_SCALE
    q_rope = rope_rows(q[..., MLA_NOPE:], pos) * MLA_SCALE
    q_cat = jnp.concatenate([q_abs, q_rope], axis=-1).astype(BF16)
    ckv_rows = jnp.concatenate([rmsnorm(ckv, g_kv), rope_rows(kr, pos)], axis=-1)
    nq = (nq.reshape(-1, NSA_KV_HEADS, NSA_GROUP, NSA_HEAD_DIM) * NSA_SCALE).astype(BF16)
    gates = jax.nn.sigmoid(gt.reshape(-1, NSA_KV_HEADS, NSA_GROUP, 3))
    return q_cat, ckv_rows, nq, kvc, kvs, kvw, gates


def chunk_partials(rows, w):
    B, L = rows.shape[:2]
    rows = pad_rows(rows, L + (-L) % CMP_STRIDE)
    ch = rows.reshape((B, rows.shape[1] // CMP_STRIDE, CMP_STRIDE) + rows.shape[2:])
    first = jnp.einsum('bcjegd,ejg->bcegd', ch, w[:, :CMP_STRIDE], precision=lax.Precision.HIGHEST)
    second = jnp.einsum('bcjegd,ejg->bcegd', ch, w[:, CMP_STRIDE:], precision=lax.Precision.HIGHEST)
    return first, second


def _cmp_weight_rows(w_cmp):
    w = jnp.transpose(w_cmp, (1, 0, 2))
    w = jnp.broadcast_to(w[..., None], w.shape + (NSA_HEAD_DIM,)).reshape(CMP_BLOCK, NSA_KV_W)
    return w.reshape(2, CMP_STRIDE, NSA_KV_W)


def even_mixer_prompt(q_cat, ckv_rows, nq, kvc, kvs, kvw, gates, w_up, w_cmp, B, T):
    slopes = alibi_slopes()
    o_m = mla_prompt(jnp.transpose(q_cat.reshape(B, T, MLA_HEADS, MLA_ROW), (0, 2, 1, 3)),
                     ckv_rows.reshape(B, T, MLA_ROW).astype(BF16), w_up)
    q = jnp.transpose(nq.reshape(B, T, NSA_KV_HEADS, NSA_GROUP, NSA_HEAD_DIM), (0, 2, 3, 1, 4))
    kv5 = (B, T, 2, NSA_KV_HEADS, NSA_HEAD_DIM)
    first, second = chunk_partials(kvc.reshape(kv5), w_cmp)
    kc = first[:, :-1] + second[:, 1:]
    n_cmp = kc.shape[1]
    n_pad = -(-n_cmp // 128) * 128
    kc = jnp.transpose(pad_rows(kc, n_pad), (2, 0, 3, 1, 4)).astype(BF16)
    o_c, sel = nsa_cmp_prompt(q, kc[0], kc[1], slopes, n_cmp)
    nsb = sel.shape[-1]
    q_sel = jnp.concatenate([q, jnp.broadcast_to((1.0 - sel).astype(BF16)[:, :, None], q.shape[:4] + (nsb,))], axis=-1)
    ks = jnp.transpose(kvs.reshape(kv5), (2, 0, 3, 1, 4)).astype(BF16)
    own_blk = (jnp.arange(T)[:, None] // SEL_BLOCK == jnp.arange(nsb)[None, :])
    k_mask = jnp.where(own_blk, MASK_FILL, 0.0).astype(BF16)
    k_sel = jnp.concatenate([ks[0], jnp.broadcast_to(k_mask, ks[0].shape[:3] + (nsb,))], axis=-1)
    o_s = nsa_flash_prompt(q_sel, k_sel, ks[1], slopes, n_band=0)
    kw = jnp.transpose(kvw.reshape(kv5), (2, 0, 3, 1, 4)).astype(BF16)
    o_w = nsa_flash_prompt(q, kw[0], kw[1], slopes, n_band=WINDOW // ATT_TILE + 1)
    gt = gates.reshape(B, T, NSA_HEADS, 3)

    def gated(o, k):
        return o.reshape(B, T, NSA_HEADS, NSA_HEAD_DIM) * gt[..., k:k + 1]

    o_n = (gated(o_c, 0) + gated(o_s, 1) + gated(o_w, 2)).reshape(B, T, NSA_HEADS * NSA_HEAD_DIM)
    return jnp.concatenate([o_m, o_n], axis=-1)


def even_mixer_sample(q_cat, ckv_rows, nq, kvc, kvs, kvw, gates, w_up, w_cmp, page_table,
                      c_mla, c_cmp, c_sel, c_win, DB, DS):
    n_pages = page_table.shape[1]
    past = n_pages * PAGE_SIZE
    slopes = alibi_slopes()
    n_pad = CMP_STRIDE

    def pad_new(a):
        return pad_rows(a.reshape(DB, DS, -1), n_pad)

    q_m = jnp.transpose(q_cat.reshape(DB, DS, MLA_HEADS, MLA_ROW), (0, 2, 1, 3)).reshape(DB, MLA_HEADS * DS, MLA_ROW)
    o_m = mla_sample(page_table, q_m, pad_new(ckv_rows).astype(BF16), w_up, c_mla)
    rows = NSA_KV_HEADS * DS * NSA_GROUP
    q = jnp.transpose(nq.reshape(DB, DS, NSA_KV_HEADS, NSA_GROUP, NSA_HEAD_DIM), (0, 2, 1, 3, 4))
    eye = jnp.eye(NSA_KV_HEADS, dtype=BF16)
    q_bd = (q[:, :, :, :, None, :] * eye[None, :, None, None, :, None]).reshape(DB, rows, NSA_KV_HALF)
    slope_rows = jnp.broadcast_to(slopes[:, None, :], (NSA_KV_HEADS, DS, NSA_GROUP)).reshape(rows, 1)
    o_c, sel = nsa_cmp_sample(page_table, q_bd, pad_new(kvc), _cmp_weight_rows(w_cmp), slope_rows,
                              c_cmp.reshape(c_cmp.shape[0], PAGE_SIZE, NSA_KV_W), DS)
    npb = past // SEL_BLOCK
    unsel = jnp.repeat((1.0 - sel[..., :npb]).astype(BF16), NSA_GROUP, axis=1)
    q2 = jnp.concatenate([q_bd, unsel], axis=-1)
    selnew = jnp.repeat(sel[..., npb:npb + 1], NSA_GROUP, axis=1)
    own_blk = (jnp.arange(past)[:, None] // SEL_BLOCK == jnp.arange(npb)[None, :])
    kmask = jnp.where(own_blk, NEG_BIG, 0.0).astype(BF16)
    g_rows = jnp.transpose(gates.reshape(DB, DS, NSA_KV_HEADS, NSA_GROUP, 3), (0, 2, 1, 3, 4)).reshape(DB, rows, 3)
    o_n = nsa_sel_sample(page_table, q2, selnew, kmask, pad_new(kvs), c_win.reshape(DB, -1, NSA_KV_W),
                         pad_new(kvw), slope_rows, o_c, g_rows,
                         c_sel.reshape(c_sel.shape[0], PAGE_SIZE, NSA_KV_W), DS)
    o_n = jnp.transpose(o_n.reshape(DB, NSA_KV_HEADS, DS, NSA_GROUP, NSA_HEAD_DIM), (0, 2, 1, 3, 4))
    return jnp.concatenate([o_m, o_n.reshape(DB, DS, NSA_HEADS * NSA_HEAD_DIM)], axis=-1)


DN_REP = DN_V_HEADS // DN_QK_HEADS
DN_QK_W = DN_QK_HEADS * DN_K_DIM
DN_V_W = DN_V_HEADS * DN_V_DIM
_BNN = (((2,), (1,)), ((0,), (0,)))
_BNT = (((2,), (2,)), ((0,), (0,)))
_BTN = (((1,), (1,)), ((0,), (0,)))


def _bdot(a, b, dims=_BNN):
    return lax.dot_general(a, b, dims, preferred_element_type=F32)


def _dot3(a, b):
    a_hi = a.astype(BF16)
    a_lo = (a - a_hi.astype(F32)).astype(BF16)
    b_hi = b.astype(BF16)
    b_lo = (b - b_hi.astype(F32)).astype(BF16)
    return _bdot(a_hi, b_hi) + (_bdot(a_hi, b_lo) + _bdot(a_lo, b_hi))


def _softplus(x):
    return jnp.maximum(x, 0.0) + jnp.log(1.0 + jnp.exp(-jnp.abs(x)))


def _silu(x):
    return x * jax.nn.sigmoid(x)


def _l2norm(x):
    return x * lax.rsqrt(jnp.sum(x * x, axis=-1, keepdims=True) + L2_EPS)


def _short_conv(prev, x, w):
    n = x.shape[0]
    xp = jnp.concatenate([prev, x], axis=0)
    acc = xp[8:8 + n] * w[DN_CONV - 1:DN_CONV]
    for j in range(DN_CONV - 1):
        off = 8 - (DN_CONV - 1) + j
        acc = acc + xp[off:off + n] * w[j:j + 1]
    return _silu(acc)


def _pad_conv_buf(buf):
    return jnp.concatenate([jnp.zeros((8 - buf.shape[0], buf.shape[1]), F32), buf], axis=0)


def _chunk_terms(q, k, v, kk, qk, g_row, beta_row):
    c = q.shape[1]
    ii = lax.broadcasted_iota(jnp.int32, (1, c, c), 1)
    jj = lax.broadcasted_iota(jnp.int32, (1, c, c), 2)
    eye = ii == jj
    causal = ii >= jj
    g_col = jnp.sum(jnp.where(eye, g_row, 0.0), axis=2, keepdims=True)
    beta_col = jnp.sum(jnp.where(eye, beta_row, 0.0), axis=2, keepdims=True)
    gc_col = jnp.sum(jnp.where(causal, g_row, 0.0), axis=2, keepdims=True)
    gc_row = jnp.sum(jnp.where(ii <= jj, g_col, 0.0), axis=1, keepdims=True)
    decay = jnp.exp(jnp.where(causal, gc_col - gc_row, -jnp.inf))
    n_mat = jnp.where(ii > jj, -(kk * beta_col * decay), 0.0)
    t_mat = jnp.where(eye, 1.0, n_mat)
    p = n_mat
    for _ in range(int(np.log2(c)) - 1):
        p = _dot3(p, p)
        t_mat = t_mat + _dot3(t_mat, p)
    e_col = jnp.exp(gc_col)
    sol = _dot3(t_mat, jnp.concatenate([v * beta_col, k * (beta_col * e_col)], axis=-1))
    gc_last = gc_col[:, c - 1:c]
    return (sol[..., :v.shape[-1]], sol[..., v.shape[-1]:], q * e_col, k * jnp.exp(gc_last - gc_col), qk * decay,
            jnp.exp(gc_last))


def _state_step(s, u, w, qd, kd, attn, gl):
    s_b = s.astype(BF16)
    v_new = u - _bdot(w.astype(BF16), s_b)
    v_b = v_new.astype(BF16)
    o = _bdot(qd.astype(BF16), s_b) + _bdot(attn.astype(BF16), v_b)
    s = s * gl + _bdot(kd.astype(BF16), v_b, _BTN)
    return o, s


def _gated_out(o, z, g_norm):
    ms = jnp.mean(o * o, axis=-1, keepdims=True)
    return o * lax.rsqrt(ms + NORM_EPS) * g_norm * _silu(z)


def _gdn_chunk_kernel(b_ref, a_ref, alog_ref, dtb_ref, xq_ref, hq_ref, xk_ref, hk_ref, xv_ref, hv_ref,
                      wq_ref, wk_ref, wv_ref, bq_ref, bk_ref, bv_ref,
                      u_ref, w_ref, qd_ref, kd_ref, at_ref, gl_ref, *, chunk):
    i = pl.program_id(2)
    first = i == 0
    rows = xq_ref.shape[0]

    def conv(x_ref, halo_ref, buf_ref, w_ref_):
        prev = jnp.where(first, _pad_conv_buf(buf_ref[...]), halo_ref[...])
        return _short_conv(prev, x_ref[...], w_ref_[...])

    q = _l2norm(conv(xq_ref, hq_ref, bq_ref, wq_ref)) * DN_K_DIM ** -0.5
    k = _l2norm(conv(xk_ref, hk_ref, bk_ref, wk_ref))
    v = conv(xv_ref, hv_ref, bv_ref, wv_ref)
    nc = rows // chunk
    q3 = q.reshape(nc, chunk, DN_K_DIM)
    k3 = k.reshape(nc, chunk, DN_K_DIM)
    kb = k3.astype(BF16)
    kk = _bdot(kb, kb, _BNT)
    qk = _bdot(q3.astype(BF16), kb, _BNT)
    rep = lambda a: jnp.concatenate([a] * DN_REP, axis=0)
    v3 = jnp.concatenate([v[:, j * DN_V_DIM:(j + 1) * DN_V_DIM].reshape(nc, chunk, DN_V_DIM)
                          for j in range(DN_REP)], axis=0)
    g_row = jnp.concatenate([-jnp.exp(alog_ref[j]) * _softplus(a_ref[j] + dtb_ref[j]) for j in range(DN_REP)], axis=0)
    beta_row = jnp.concatenate([jax.nn.sigmoid(b_ref[j]) for j in range(DN_REP)], axis=0)
    u, w, qd, kd, attn, gl = _chunk_terms(rep(q3), rep(k3), v3, rep(kk), rep(qk), g_row, beta_row)
    for j in range(DN_REP):
        part = lambda a: a[j * nc:(j + 1) * nc].reshape(rows, a.shape[-1])
        u_ref[j] = part(u)
        w_ref[j] = part(w).astype(BF16)
        qd_ref[j] = part(qd).astype(BF16)
        kd_ref[j] = part(kd).astype(BF16)
        at_ref[j] = part(attn).astype(BF16)
        gl_ref[j] = jnp.broadcast_to(gl[j * nc:(j + 1) * nc], gl_ref.shape[1:])


def gdn_chunk_prompt(proj, ba_rows, conv_buf, w_conv, a_log, dt_bias, B, T, row_block=512):
    c = DN_CHUNK
    n_ch = T // c
    row_block = min(row_block, T)
    nb = T // row_block
    nc = row_block // c
    qk_blk = DN_QK_W // DN_K_DIM
    rep_w = DN_REP * DN_V_DIM

    def halo(b_, i):
        return jnp.maximum((b_ * T + i * row_block) // 8 - 1, 0)

    def x_specs(width, col):
        return [pl.BlockSpec((row_block, width), lambda b_, h, i: (b_ * nb + i, col(h))),
                pl.BlockSpec((8, width), lambda b_, h, i: (halo(b_, i), col(h)))]

    gate_spec = lambda off: pl.BlockSpec((None, DN_REP, nc, 1, c), lambda b_, h, i: (b_, off + h, i, 0, 0))
    head_spec = pl.BlockSpec((DN_REP, 1, 1), lambda b_, h, i: (h, 0, 0))
    per_head = lambda width, dt: jax.ShapeDtypeStruct((B, DN_V_HEADS, T, width), dt)
    out_spec = lambda width: pl.BlockSpec((None, DN_REP, row_block, width), lambda b_, h, i: (b_, h, i, 0))
    v_col = lambda h: DN_CONV_DIM // rep_w - DN_V_W // rep_w + h
    return pl.pallas_call(
        functools.partial(_gdn_chunk_kernel, chunk=c),
        grid=(B, DN_QK_HEADS, nb),
        in_specs=[gate_spec(0), gate_spec(DN_V_HEADS // DN_REP), head_spec, head_spec]
        + x_specs(DN_K_DIM, lambda h: h) + x_specs(DN_K_DIM, lambda h: qk_blk + h) + x_specs(rep_w, v_col)
        + [pl.BlockSpec((DN_CONV, DN_K_DIM), lambda b_, h, i: (0, h)),
           pl.BlockSpec((DN_CONV, DN_K_DIM), lambda b_, h, i: (0, qk_blk + h)),
           pl.BlockSpec((DN_CONV, rep_w), lambda b_, h, i: (0, v_col(h))),
           pl.BlockSpec((None, DN_CONV - 1, DN_K_DIM), lambda b_, h, i: (b_, 0, h)),
           pl.BlockSpec((None, DN_CONV - 1, DN_K_DIM), lambda b_, h, i: (b_, 0, qk_blk + h)),
           pl.BlockSpec((None, DN_CONV - 1, rep_w), lambda b_, h, i: (b_, 0, v_col(h)))],
        out_specs=[out_spec(DN_V_DIM), out_spec(DN_K_DIM), out_spec(DN_K_DIM), out_spec(DN_K_DIM), out_spec(c),
                   pl.BlockSpec((None, DN_REP, nc, 1, 128), lambda b_, h, i: (b_, h, i, 0, 0))],
        out_shape=[per_head(DN_V_DIM, F32), per_head(DN_K_DIM, BF16), per_head(DN_K_DIM, BF16),
                   per_head(DN_K_DIM, BF16), per_head(c, BF16),
                   jax.ShapeDtypeStruct((B, DN_V_HEADS, n_ch, 1, 128), F32)],
        compiler_params=_params("arbitrary", "arbitrary", "arbitrary"),
        name="gdn_chunk_prompt",
    )(ba_rows, ba_rows, a_log.reshape(DN_V_HEADS, 1, 1), dt_bias.reshape(DN_V_HEADS, 1, 1),
      proj, proj, proj, proj, proj, proj, w_conv, w_conv, w_conv, conv_buf, conv_buf, conv_buf)


def _gdn_scan_kernel(s0_ref, u_ref, w_ref, qd_ref, kd_ref, at_ref, gl_ref, z_ref, gn_ref, o_ref, s_ref, *, chunk):
    n_heads = s0_ref.shape[0]
    n_ch = u_ref.shape[1] // chunk

    @pl.when(pl.program_id(2) == 0)
    def _():
        s_ref[...] = s0_ref[...]

    g_norm = gn_ref[...]
    for ci in range(n_ch):
        rows = slice(ci * chunk, (ci + 1) * chunk)
        o, s = _state_step(s_ref[...], u_ref[:, rows, :], w_ref[:, rows, :], qd_ref[:, rows, :], kd_ref[:, rows, :],
                           at_ref[:, rows, :], gl_ref[:, ci])
        s_ref[...] = s
        for j in range(n_heads):
            cols = slice(j * DN_V_DIM, (j + 1) * DN_V_DIM)
            o_ref[rows, cols] = _gated_out(o[j], z_ref[rows, cols], g_norm)


def gdn_scan_prompt(s0, terms, proj, g_norm, B, T, heads_per_step=8, row_block=512):
    u, w, qd, kd, attn, gl = terms
    hb = heads_per_step
    row_block = min(row_block, T)
    nb = T // row_block
    nc = row_block // DN_CHUNK
    z_col = DN_CONV_DIM // (hb * DN_V_DIM)
    per_head = lambda width: pl.BlockSpec((None, hb, row_block, width), lambda b_, h, i: (b_, h, i, 0))
    state_spec = pl.BlockSpec((None, hb, DN_K_DIM, DN_V_DIM), lambda b_, h, i: (b_, h, 0, 0))
    return pl.pallas_call(
        functools.partial(_gdn_scan_kernel, chunk=DN_CHUNK),
        grid=(B, DN_V_HEADS // hb, nb),
        in_specs=[state_spec, per_head(DN_V_DIM), per_head(DN_K_DIM), per_head(DN_K_DIM), per_head(DN_K_DIM),
                  per_head(DN_CHUNK), pl.BlockSpec((None, hb, nc, 1, 128), lambda b_, h, i: (b_, h, i, 0, 0)),
                  pl.BlockSpec((row_block, hb * DN_V_DIM), lambda b_, h, i: (b_ * nb + i, z_col + h)),
                  pl.BlockSpec((1, DN_V_DIM), lambda b_, h, i: (0, 0))],
        out_specs=[pl.BlockSpec((None, row_block, hb * DN_V_DIM), lambda b_, h, i: (b_, i, h)), state_spec],
        out_shape=[jax.ShapeDtypeStruct((B, T, DN_V_W), F32),
                   jax.ShapeDtypeStruct((B, DN_V_HEADS, DN_K_DIM, DN_V_DIM), F32)],
        compiler_params=_params("arbitrary", "arbitrary", "arbitrary"),
        name="gdn_scan_prompt",
    )(s0, u, w, qd, kd, attn, gl, proj, g_norm.reshape(1, DN_V_DIM))


def _gdn_sample_kernel(x_ref, buf_ref, wc_ref, bar_ref, alog_c_ref, dtb_c_ref, s0_ref, gn_ref, o_ref, s_ref):
    x = x_ref[...]
    conv = _short_conv(_pad_conv_buf(buf_ref[...]), x[:, :DN_CONV_DIM], wc_ref[...])
    bar = bar_ref[...]
    g_row = -jnp.exp(alog_c_ref[...]) * _softplus(bar[DN_V_HEADS:] + dtb_c_ref[...])
    beta_row = jax.nn.sigmoid(bar[:DN_V_HEADS])
    g_norm = gn_ref[...]

    def heads(col0, n_heads, width, rep):
        return jnp.stack([conv[:, col0 + h * width:col0 + (h + 1) * width]
                          for h in range(n_heads) for _ in range(rep)], axis=0)

    q = _l2norm(heads(0, DN_QK_HEADS, DN_K_DIM, DN_REP)) * DN_K_DIM ** -0.5
    k = _l2norm(heads(DN_QK_W, DN_QK_HEADS, DN_K_DIM, DN_REP))
    v = heads(2 * DN_QK_W, DN_V_HEADS, DN_V_DIM, 1)
    kb = k.astype(BF16)
    u, w, qd, kd, attn, gl = _chunk_terms(q, k, v, _bdot(kb, kb, _BNT), _bdot(q.astype(BF16), kb, _BNT),
                                          g_row, beta_row)
    o, s = _state_step(s0_ref[...], u, w, qd, kd, attn, gl)
    s_ref[...] = s
    for hv in range(DN_V_HEADS):
        z = x[:, DN_CONV_DIM + hv * DN_V_DIM:DN_CONV_DIM + (hv + 1) * DN_V_DIM]
        o_ref[:, hv * DN_V_DIM:(hv + 1) * DN_V_DIM] = _gated_out(o[hv], z, g_norm)


def gdn_sample(proj, row0, ba_rows, conv_buf, s0, w_conv, a_log, dt_bias, g_norm, DB, DS):
    assert DS == 8 and row0 % DS == 0
    n_main = proj.shape[1]
    full = lambda shape: pl.BlockSpec(shape, lambda b: (0,) * len(shape))
    state_spec = pl.BlockSpec((None, DN_V_HEADS, DN_K_DIM, DN_V_DIM), lambda b: (b, 0, 0, 0))
    return pl.pallas_call(
        _gdn_sample_kernel,
        grid=(DB,),
        in_specs=[pl.BlockSpec((DS, n_main), lambda b: (row0 // DS + b, 0)),
                  pl.BlockSpec((None, DN_CONV - 1, DN_CONV_DIM), lambda b: (b, 0, 0)),
                  full((DN_CONV, DN_CONV_DIM)),
                  pl.BlockSpec((None, 2 * DN_V_HEADS, 1, DS), lambda b: (b, 0, 0, 0)),
                  full((DN_V_HEADS, 1, 1)), full((DN_V_HEADS, 1, 1)),
                  state_spec, full((1, DN_V_DIM))],
        out_specs=[pl.BlockSpec((None, DS, DN_V_W), lambda b: (b, 0, 0)), state_spec],
        out_shape=[jax.ShapeDtypeStruct((DB, DS, DN_V_W), F32),
                   jax.ShapeDtypeStruct((DB, DN_V_HEADS, DN_K_DIM, DN_V_DIM), F32)],
        compiler_params=_params("arbitrary"),
        name="gdn_sample",
    )(proj, conv_buf, w_conv, ba_rows, a_log.reshape(-1, 1, 1), dt_bias.reshape(-1, 1, 1), s0,
      g_norm.reshape(1, DN_V_DIM))


def delta_mixers(proj, proj_ba, conv_state, s0_sample, w_conv, a_log, dt_bias, g_norm, B, T, DB, DS):
    n_p = B * T
    assert T % DN_CHUNK == 0 and T >= DN_CONV - 1 and DS >= DN_CONV - 1
    n_ch = T // DN_CHUNK
    ba_p = jnp.transpose(proj_ba[:n_p].reshape(B, n_ch, DN_CHUNK, -1), (0, 3, 1, 2))[:, :, :, None, :]
    buf0 = jnp.zeros((B, DN_CONV - 1, DN_CONV_DIM), F32)
    s0 = jnp.zeros((B, DN_V_HEADS, DN_K_DIM, DN_V_DIM), F32)
    terms = gdn_chunk_prompt(proj, ba_p, buf0, w_conv, a_log, dt_bias, B, T)
    o_p, s_p = gdn_scan_prompt(s0, terms, proj, g_norm, B, T)
    x_p = proj[:n_p].reshape(B, T, -1)
    c_p = x_p[:, T - (DN_CONV - 1):, :DN_CONV_DIM]
    ba_s = jnp.transpose(proj_ba[n_p:].reshape(DB, DS, -1), (0, 2, 1))[:, :, None, :]
    o_s, s_s = gdn_sample(proj, n_p, ba_s, conv_state, s0_sample, w_conv, a_log, dt_bias, g_norm, DB, DS)
    x_s = proj[n_p:].reshape(DB, DS, -1)
    c_s = x_s[:, DS - (DN_CONV - 1):, :DN_CONV_DIM]
    return o_p, c_p, s_p, o_s, c_s, s_s


def kernel(x_prompt, x_sample, cache_mla, cache_nsa_cmp, cache_nsa_sel, cache_nsa_win, state_delta,
           state_delta_conv, page_table, norm_g, w_ffn_in, w_ffn_out, w_in_even, g_q_lora, w_q_up,
           g_kv_lora, w_kv_up, w_cmp, w_o_even, w_in_odd, w_conv, a_log, dt_bias, g_dn_norm, w_out_odd):
    B, T, D = x_prompt.shape
    DB, DS, _ = x_sample.shape
    n_p = B * T
    n_s = DB * DS
    past = page_table.shape[1] * PAGE_SIZE
    x = jnp.concatenate([x_prompt.reshape(n_p, D), x_sample.reshape(n_s, D)], axis=0)
    pos = jnp.concatenate([jnp.tile(jnp.arange(T), B), jnp.tile(past + jnp.arange(DS), DB)])
    kv5 = (2, NSA_KV_HEADS, NSA_HEAD_DIM)

    mla_p, mla_s, cmp_p, cmp_s, sel_p, sel_s, win_p, win_s = [], [], [], [], [], [], [], []
    dst_p, dst_s, dcv_p, dcv_s = [], [], [], []
    for l in range(DEPTH):
        ng = norm_g[l]
        i = l // 2
        x = ffn_half(x, ng[0], ng[1], w_ffn_in[l, 0].astype(BF16), w_ffn_out[l, 0].astype(BF16))
        if l % 2 == 0:
            n_in = w_in_even.shape[-1]
            proj = norm_proj(x, ng[2], w_in_even[i].astype(BF16), tm=256, tn=n_in)
            q_cat, ckv_rows, nq, kvc, kvs, kvw, gates = even_project(
                proj, pos, g_q_lora[i], w_q_up[i], g_kv_lora[i], w_kv_up[i])
            w_up = jnp.transpose(w_kv_up[i][..., MLA_NOPE:], (1, 0, 2)).astype(BF16)
            op = even_mixer_prompt(q_cat[:n_p], ckv_rows[:n_p], nq[:n_p], kvc[:n_p], kvs[:n_p], kvw[:n_p],
                                   gates[:n_p], w_up, w_cmp[i], B, T)
            os_ = even_mixer_sample(q_cat[n_p:], ckv_rows[n_p:], nq[n_p:], kvc[n_p:], kvs[n_p:], kvw[n_p:],
                                    gates[n_p:], w_up, w_cmp[i], page_table, cache_mla[i], cache_nsa_cmp[i],
                                    cache_nsa_sel[i], cache_nsa_win[i], DB, DS)
            mla_p.append(ckv_rows[:n_p].reshape(B, T, MLA_ROW))
            mla_s.append(ckv_rows[n_p:].reshape(DB, DS, MLA_ROW))
            cmp_p.append(kvc[:n_p].reshape((B, T) + kv5)); cmp_s.append(kvc[n_p:].reshape((DB, DS) + kv5))
            sel_p.append(kvs[:n_p].reshape((B, T) + kv5)); sel_s.append(kvs[n_p:].reshape((DB, DS) + kv5))
            kvw_p = kvw[:n_p].reshape((B, T) + kv5)
            win_p.append(kvw_p[:, T - min(WINDOW, T):])
            win_s.append(jnp.concatenate([cache_nsa_win[i], kvw[n_p:].reshape((DB, DS) + kv5)], axis=1)[:, DS:])
            w_o = w_o_even[i]
        else:
            n_main = DN_CONV_DIM + DN_V_HEADS * DN_V_DIM
            w_in = w_in_odd[i]
            proj_main = norm_proj(x, ng[2], w_in[:, :n_main].astype(BF16), tm=512, tn=1024)
            proj_ba = norm_proj(x, ng[2], w_in[:, n_main:].astype(BF16), tm=512, tn=2 * DN_V_HEADS)
            op, c_p, S_p, os_, c_s, S_s = delta_mixers(
                proj_main, proj_ba, state_delta_conv[i], state_delta[i], w_conv[i], a_log[i], dt_bias[i],
                g_dn_norm[i], B, T, DB, DS)
            dst_p.append(S_p.astype(state_delta.dtype)); dst_s.append(S_s.astype(state_delta.dtype))
            dcv_p.append(c_p); dcv_s.append(c_s)
            w_o = w_out_odd[i]
        o = jnp.concatenate([op.reshape(n_p, -1), os_.reshape(n_s, -1)], axis=0)
        x = out_proj_residual(o, w_o.astype(BF16), x, ng[3])
        x = ffn_half(x, ng[4], ng[5], w_ffn_in[l, 1].astype(BF16), w_ffn_out[l, 1].astype(BF16))
    return (x[:n_p].reshape(B, T, D), x[n_p:].reshape(DB, DS, D),
            jnp.stack(mla_p), jnp.stack(mla_s),
            jnp.stack(cmp_p), jnp.stack(cmp_s),
            jnp.stack(sel_p), jnp.stack(sel_s),
            jnp.stack(win_p), jnp.stack(win_s),
            jnp.stack(dst_p), jnp.stack(dst_s),
            jnp.stack(dcv_p), jnp.stack(dcv_s))
```

```python
import functools

import jax
import jax.numpy as jnp
import numpy as np
from jax import lax
from jax.experimental import pallas as pl
from jax.experimental.pallas import tpu as pltpu

D_MODEL = 2048
DEPTH = 4
PAGE_SIZE = 128
MLA_HEADS = 8
MLA_Q_LORA = 512
MLA_KV_LORA = 256
MLA_NOPE = 128
MLA_ROPE = 64
MLA_V = 128
MLA_ROW = MLA_KV_LORA + MLA_ROPE
ROPE_THETA = 10000.0
MLA_SCALE = (MLA_NOPE + MLA_ROPE) ** -0.5
NSA_HEADS = 16
NSA_KV_HEADS = 2
NSA_GROUP = NSA_HEADS // NSA_KV_HEADS
NSA_HEAD_DIM = 64
NSA_SCALE = NSA_HEAD_DIM ** -0.5
CMP_STRIDE = 16
CMP_BLOCK = 2 * CMP_STRIDE
SEL_BLOCK = 64
SEL_TOP = 16
SEL_RATIO = SEL_BLOCK // CMP_STRIDE
WINDOW = 512
FORCE_BONUS = 1.0e4
DN_QK_HEADS = 16
DN_V_HEADS = 32
DN_K_DIM = 128
DN_V_DIM = 128
DN_CONV = 4
DN_CHUNK = 64
DN_CONV_DIM = 2 * DN_QK_HEADS * DN_K_DIM + DN_V_HEADS * DN_V_DIM
D_FF = 5632
NORM_EPS = 1e-6
L2_EPS = 1e-6
NEG_BIG = -1e30
MASK_FILL = 2.0 * NEG_BIG
TINY = 1e-30
NSA_KV_W = 2 * NSA_KV_HEADS * NSA_HEAD_DIM
NSA_KV_HALF = NSA_KV_HEADS * NSA_HEAD_DIM

F32 = jnp.float32
BF16 = jnp.bfloat16

VMEM_LIMIT_BYTES = 56 * 1024 * 1024
ATT_TILE = 256

_NT = (((1,), (1,)), ((), ()))


def _params(*sem):
    return pltpu.CompilerParams(dimension_semantics=sem, vmem_limit_bytes=VMEM_LIMIT_BYTES)


def _ffn_kernel(x_ref, gpre_ref, gpost_ref, wg_ref, wu_ref, wo_ref, o_ref, h_scr, acc_scr):
    j = pl.program_id(1)

    @pl.when(j == 0)
    def _():
        x = x_ref[...]
        ms = jnp.mean(x * x, axis=-1, keepdims=True)
        h_scr[...] = (x * lax.rsqrt(ms + NORM_EPS) * gpre_ref[...]).astype(BF16)
        acc_scr[...] = jnp.zeros_like(acc_scr)

    h = h_scr[...]
    gate = jnp.dot(h, wg_ref[...], preferred_element_type=F32)
    up = jnp.dot(h, wu_ref[...], preferred_element_type=F32)
    a = (gate * jax.nn.sigmoid(gate) * up).astype(BF16)
    acc_scr[...] += jnp.dot(a, wo_ref[...], preferred_element_type=F32)

    @pl.when(j == pl.num_programs(1) - 1)
    def _():
        y = acc_scr[...]
        ms = jnp.mean(y * y, axis=-1, keepdims=True)
        o_ref[...] = x_ref[...] + 0.5 * (y * lax.rsqrt(ms + NORM_EPS) * gpost_ref[...])


def ffn_half(x, g_pre, g_post, w_in, w_out, tm=512, tf=512):
    m, d = x.shape
    f = w_out.shape[0]
    nf = f // tf
    assert m % tm == 0 and f % tf == 0
    return pl.pallas_call(
        _ffn_kernel,
        grid=(m // tm, nf),
        in_specs=[
            pl.BlockSpec((tm, d), lambda i, j: (i, 0)),
            pl.BlockSpec((1, d), lambda i, j: (0, 0)),
            pl.BlockSpec((1, d), lambda i, j: (0, 0)),
            pl.BlockSpec((d, tf), lambda i, j: (0, j)),
            pl.BlockSpec((d, tf), lambda i, j: (0, j + nf)),
            pl.BlockSpec((tf, d), lambda i, j: (j, 0)),
        ],
        out_specs=pl.BlockSpec((tm, d), lambda i, j: (i, 0)),
        out_shape=jax.ShapeDtypeStruct((m, d), F32),
        scratch_shapes=[pltpu.VMEM((tm, d), BF16), pltpu.VMEM((tm, d), F32)],
        compiler_params=_params("arbitrary", "arbitrary"),
        name="ffn_half",
    )(x, g_pre.reshape(1, d), g_post.reshape(1, d), w_in, w_in, w_out)


def _norm_proj_kernel(x_ref, g_ref, w_ref, o_ref, h_scr):
    @pl.when(pl.program_id(1) == 0)
    def _():
        x = x_ref[...]
        ms = jnp.mean(x * x, axis=-1, keepdims=True)
        h_scr[...] = (x * lax.rsqrt(ms + NORM_EPS) * g_ref[...]).astype(BF16)

    o_ref[...] = jnp.dot(h_scr[...], w_ref[...], preferred_element_type=F32)


def norm_proj(x, g, w, tm, tn):
    m, d = x.shape
    n = w.shape[1]
    assert m % tm == 0 and n % tn == 0
    return pl.pallas_call(
        _norm_proj_kernel,
        grid=(m // tm, n // tn),
        in_specs=[
            pl.BlockSpec((tm, d), lambda i, j: (i, 0)),
            pl.BlockSpec((1, d), lambda i, j: (0, 0)),
            pl.BlockSpec((d, tn), lambda i, j: (0, j)),
        ],
        out_specs=pl.BlockSpec((tm, tn), lambda i, j: (i, j)),
        out_shape=jax.ShapeDtypeStruct((m, n), F32),
        scratch_shapes=[pltpu.VMEM((tm, d), BF16)],
        compiler_params=_params("arbitrary", "arbitrary"),
        name="norm_proj",
    )(x, g.reshape(1, d), w)


def _out_proj_kernel(a_ref, w_ref, x_ref, g_ref, o_ref):
    y = jnp.dot(a_ref[...].astype(BF16), w_ref[...], preferred_element_type=F32)
    ms = jnp.mean(y * y, axis=-1, keepdims=True)
    o_ref[...] = x_ref[...] + y * lax.rsqrt(ms + NORM_EPS) * g_ref[...]


def out_proj_residual(a, w, x, g, tm=256):
    m, k = a.shape
    d = w.shape[1]
    assert m % tm == 0
    return pl.pallas_call(
        _out_proj_kernel,
        grid=(m // tm,),
        in_specs=[
            pl.BlockSpec((tm, k), lambda i: (i, 0)),
            pl.BlockSpec((k, d), lambda i: (0, 0)),
            pl.BlockSpec((tm, d), lambda i: (i, 0)),
            pl.BlockSpec((1, d), lambda i: (0, 0)),
        ],
        out_specs=pl.BlockSpec((tm, d), lambda i: (i, 0)),
        out_shape=jax.ShapeDtypeStruct((m, d), F32),
        compiler_params=_params("arbitrary"),
        name="out_proj_residual",
    )(a, w, x, g.reshape(1, d))


def rmsnorm(x, g):
    xf = x.astype(F32)
    y = xf * lax.rsqrt(jnp.mean(xf * xf, axis=-1, keepdims=True) + NORM_EPS)
    return (y * g.astype(F32)).astype(x.dtype)


def l2norm(x):
    xf = x.astype(F32)
    return xf * lax.rsqrt(jnp.sum(xf * xf, axis=-1, keepdims=True) + L2_EPS)


def rope_rows(x, pos):
    half = x.shape[-1] // 2
    inv = ROPE_THETA ** (-2.0 * jnp.arange(half, dtype=F32) / x.shape[-1])
    ang = pos.astype(F32)[:, None] * inv
    shape = (pos.shape[0],) + (1,) * (x.ndim - 2) + (half,)
    cos = jnp.cos(ang).reshape(shape)
    sin = jnp.sin(ang).reshape(shape)
    x1 = x[..., :half]
    x2 = x[..., half:]
    return jnp.concatenate([x1 * cos - x2 * sin, x1 * sin + x2 * cos], axis=-1)


def alibi_slopes():
    h = jnp.arange(NSA_HEADS, dtype=F32) + 1.0
    return (2.0 ** (-8.0 * h / NSA_HEADS)).reshape(NSA_KV_HEADS, NSA_GROUP)


def pad_rows(x, length):
    return jnp.pad(x, [(0, 0), (0, length - x.shape[1])] + [(0, 0)] * (x.ndim - 2))


def _masked_softmax_parts(s, valid):
    s = jnp.where(valid, s, NEG_BIG)
    m = jnp.max(s, axis=-1, keepdims=True)
    e = jnp.where(valid, jnp.exp(s - m), 0.0)
    return e, jnp.sum(e, axis=-1, keepdims=True)


def _topk_mask(score, k):
    n = score.shape[-1]
    lane = lax.broadcasted_iota(jnp.int32, score.shape, score.ndim - 1)
    sel = jnp.zeros(score.shape, F32)
    x = score
    for _ in range(k):
        m = jnp.max(x, axis=-1, keepdims=True)
        idx = jnp.min(jnp.where(x == m, lane, n), axis=-1, keepdims=True)
        hit = lane == idx
        sel = jnp.where(hit & (m > -jnp.inf), 1.0, sel)
        x = jnp.where(hit, -jnp.inf, x)
    return sel


def _block_scores(blk_imp, q_pos):
    blk = lax.broadcasted_iota(jnp.int32, blk_imp.shape, 1)
    cur = q_pos // SEL_BLOCK
    valid = blk <= cur
    forced = valid & ((blk == 0) | (blk >= cur - 1))
    return jnp.where(valid, blk_imp + FORCE_BONUS * forced.astype(F32), -jnp.inf)


def _imp_to_block_matrix(n_cmp, n_blk):
    c = np.arange(n_cmp)[:, None]
    n = np.arange(n_blk)[None, :]
    return jnp.asarray(((c // SEL_RATIO == n) | (c == SEL_RATIO * n - 1)).astype(np.float32))


def _mla_prompt_kernel(q_ref, k_ref, wup_ref, o_ref, m_scr, l_scr, acc_scr):
    i = pl.program_id(1)
    j = pl.program_id(2)
    heads, tq, dq = q_ref.shape
    tk = k_ref.shape[0]

    @pl.when(j == 0)
    def _():
        m_scr[...] = jnp.full_like(m_scr, NEG_BIG)
        l_scr[...] = jnp.zeros_like(l_scr)
        acc_scr[...] = jnp.zeros_like(acc_scr)

    def step(diagonal):
        q = q_ref[...].reshape(heads * tq, dq)
        k = k_ref[...]
        s = lax.dot_general(q, k, _NT, preferred_element_type=F32)
        if diagonal:
            row = lax.broadcasted_iota(jnp.int32, (heads * tq, 1), 0)
            t_pos = i * tq + (row & (tq - 1))
            k_pos = j * tk + lax.broadcasted_iota(jnp.int32, (1, tk), 1)
            s = jnp.where(k_pos <= t_pos, s, MASK_FILL)
        m_old = m_scr[...]
        m_new = jnp.maximum(m_old, jnp.max(s, axis=-1, keepdims=True))
        e = jnp.exp(s - m_new)
        alpha = jnp.exp(m_old - m_new)
        l_scr[...] = alpha * l_scr[...] + jnp.sum(e, axis=-1, keepdims=True)
        acc_scr[...] = alpha * acc_scr[...] + jnp.dot(e.astype(BF16), k[:, :MLA_KV_LORA], preferred_element_type=F32)
        m_scr[...] = m_new

    @pl.when(j < i)
    def _():
        step(False)

    @pl.when(j == i)
    def _():
        step(True)
        o_lat = (acc_scr[...] / jnp.maximum(l_scr[...], TINY)).astype(BF16)
        outs = [jnp.dot(o_lat[h * tq:(h + 1) * tq], wup_ref[h], preferred_element_type=F32) for h in range(heads)]
        o_ref[...] = jnp.concatenate(outs, axis=-1)


def mla_prompt(q_cat, ckv, w_up, tile=ATT_TILE):
    b, h, t, dq = q_cat.shape
    assert t % tile == 0 and tile & (tile - 1) == 0
    nt = t // tile
    return pl.pallas_call(
        _mla_prompt_kernel,
        grid=(b, nt, nt),
        in_specs=[
            pl.BlockSpec((None, h, tile, dq), lambda b_, i, j: (b_, 0, i, 0)),
            pl.BlockSpec((None, tile, dq), lambda b_, i, j: (b_, jnp.minimum(i, j), 0)),
            pl.BlockSpec((h, MLA_KV_LORA, MLA_V), lambda b_, i, j: (0, 0, 0)),
        ],
        out_specs=pl.BlockSpec((None, tile, h * MLA_V), lambda b_, i, j: (b_, i, 0)),
        out_shape=jax.ShapeDtypeStruct((b, t, h * MLA_V), F32),
        scratch_shapes=[pltpu.VMEM((h * tile, 1), F32), pltpu.VMEM((h * tile, 1), F32),
                        pltpu.VMEM((h * tile, MLA_KV_LORA), F32)],
        compiler_params=_params("arbitrary", "arbitrary", "arbitrary"),
        name="mla_prompt",
    )(q_cat, ckv, w_up)


def _page_specs(n_pages, width, layer):
    return [pl.BlockSpec((None, None, width, PAGE_SIZE), lambda b, pt, p=p: (layer, pt[b, p], 0, 0))
            for p in range(n_pages)]


def _pages_feature_major(cache):
    c = cache.reshape(cache.shape[:3] + (-1,))
    return jnp.swapaxes(c, 2, 3)


def _mla_sample_kernel(pt_ref, q_ref, new_ref, wup_ref, *rest, n_pages, n_new):
    pages = rest[:n_pages]
    o_ref = rest[n_pages]
    s_scr = rest[n_pages + 1]
    q = q_ref[...]
    rows = q.shape[0]
    heads = rows // n_new
    for p in range(n_pages):
        kp = pages[p][...].astype(BF16)
        s_scr[:, p * PAGE_SIZE:(p + 1) * PAGE_SIZE] = jnp.dot(q, kp, preferred_element_type=F32)
    new = new_ref[...]
    s_new = lax.dot_general(q, new, _NT, preferred_element_type=F32)
    t_row = lax.broadcasted_iota(jnp.int32, (rows, 1), 0) & (n_new - 1)
    valid_new = lax.broadcasted_iota(jnp.int32, (1, new.shape[0]), 1) <= t_row
    s_new = jnp.where(valid_new, s_new, NEG_BIG)
    s = s_scr[...]
    m = jnp.maximum(jnp.max(s, axis=-1, keepdims=True), jnp.max(s_new, axis=-1, keepdims=True))
    e = jnp.exp(s - m).astype(BF16)
    e_new = jnp.where(valid_new, jnp.exp(s_new - m), 0.0)
    l = jnp.sum(e.astype(F32), axis=-1, keepdims=True) + jnp.sum(e_new, axis=-1, keepdims=True)
    acc = jnp.dot(e_new.astype(BF16), new[:, :MLA_KV_LORA], preferred_element_type=F32)
    for p in range(n_pages):
        vp = pages[p][:MLA_KV_LORA, :].astype(BF16)
        acc = acc + lax.dot_general(e[:, p * PAGE_SIZE:(p + 1) * PAGE_SIZE], vp, _NT, preferred_element_type=F32)
    o_lat = (acc / jnp.maximum(l, TINY)).astype(BF16)
    outs = [jnp.dot(o_lat[h * n_new:(h + 1) * n_new], wup_ref[h], preferred_element_type=F32) for h in range(heads)]
    o_ref[...] = jnp.concatenate(outs, axis=-1)


def mla_sample(page_table, q_cat, new_rows, w_up, cache, layer):
    db, rows, dq = q_cat.shape
    n_pages = page_table.shape[1]
    heads = w_up.shape[0]
    n_new = rows // heads
    assert n_new & (n_new - 1) == 0 and n_new <= new_rows.shape[1]
    n_pad = new_rows.shape[1]
    grid_spec = pltpu.PrefetchScalarGridSpec(
        num_scalar_prefetch=1,
        grid=(db,),
        in_specs=[
            pl.BlockSpec((None, rows, dq), lambda b, pt: (b, 0, 0)),
            pl.BlockSpec((None, n_pad, dq), lambda b, pt: (b, 0, 0)),
            pl.BlockSpec((heads, MLA_KV_LORA, MLA_V), lambda b, pt: (0, 0, 0)),
        ] + _page_specs(n_pages, dq, layer),
        out_specs=pl.BlockSpec((None, n_new, heads * MLA_V), lambda b, pt: (b, 0, 0)),
        scratch_shapes=[pltpu.VMEM((rows, n_pages * PAGE_SIZE), F32)],
    )
    return pl.pallas_call(
        functools.partial(_mla_sample_kernel, n_pages=n_pages, n_new=n_new),
        grid_spec=grid_spec,
        out_shape=jax.ShapeDtypeStruct((db, n_new, heads * MLA_V), F32),
        compiler_params=_params("arbitrary"),
        name="mla_sample",
    )(page_table, q_cat, new_rows, w_up, *([cache] * n_pages))


def _nsa_cmp_prompt_kernel(q_ref, kk_ref, kv_ref, slope_ref, a_ref, o_ref, sel_ref, *, n_cmp):
    i = pl.program_id(2)
    n_heads, tq, _ = q_ref.shape
    kk = kk_ref[...]
    kv = kv_ref[...]
    n_pad = kk.shape[0]
    t_pos = i * tq + lax.broadcasted_iota(jnp.int32, (tq, 1), 0)
    c_idx = lax.broadcasted_iota(jnp.int32, (1, n_pad), 1)
    dc = t_pos - (c_idx * CMP_STRIDE + (CMP_BLOCK - 1))
    valid = (dc >= 0) & (c_idx < n_cmp)
    dcf = dc.astype(F32)
    imp = jnp.zeros((tq, n_pad), F32)
    outs = []
    for r in range(n_heads):
        s = lax.dot_general(q_ref[r], kk, _NT, preferred_element_type=F32) - slope_ref[r] * dcf
        e, l = _masked_softmax_parts(s, valid)
        p = e / jnp.maximum(l, TINY)
        imp = imp + p
        outs.append(jnp.dot(p.astype(BF16), kv, preferred_element_type=F32))
    o_ref[...] = jnp.concatenate(outs, axis=-1)
    blk_imp = jnp.dot(imp, a_ref[...], preferred_element_type=F32, precision=lax.Precision.HIGHEST)
    sel_ref[...] = _topk_mask(_block_scores(blk_imp, t_pos), min(SEL_TOP, blk_imp.shape[-1]))


def nsa_cmp_prompt(q, kc_k, kc_v, slopes, n_cmp, tile=ATT_TILE):
    b, g, r, t, dh = q.shape
    n_pad = kc_k.shape[2]
    nsb = -(-t // SEL_BLOCK)
    a_mat = _imp_to_block_matrix(n_pad, nsb)
    return pl.pallas_call(
        functools.partial(_nsa_cmp_prompt_kernel, n_cmp=n_cmp),
        grid=(b, g, t // tile),
        in_specs=[
            pl.BlockSpec((None, None, r, tile, dh), lambda b_, g_, i: (b_, g_, 0, i, 0)),
            pl.BlockSpec((None, None, n_pad, dh), lambda b_, g_, i: (b_, g_, 0, 0)),
            pl.BlockSpec((None, None, n_pad, dh), lambda b_, g_, i: (b_, g_, 0, 0)),
            pl.BlockSpec((None, r, 1, 1), lambda b_, g_, i: (g_, 0, 0, 0)),
            pl.BlockSpec((n_pad, nsb), lambda b_, g_, i: (0, 0)),
        ],
        out_specs=[
            pl.BlockSpec((None, tile, r * dh), lambda b_, g_, i: (b_, i, g_)),
            pl.BlockSpec((None, None, tile, nsb), lambda b_, g_, i: (b_, g_, i, 0)),
        ],
        out_shape=[jax.ShapeDtypeStruct((b, t, g * r * dh), F32), jax.ShapeDtypeStruct((b, g, t, nsb), F32)],
        compiler_params=_params("arbitrary", "arbitrary", "arbitrary"),
        name="nsa_cmp_prompt",
    )(q, kc_k, kc_v, slopes.reshape(g, r, 1, 1), a_mat)


def _nsa_flash_kernel(q_ref, k_ref, v_ref, slope_ref, o_ref, m_scr, l_scr, acc_scr, *, n_band):
    i = pl.program_id(2)
    jj = pl.program_id(3)
    n_heads, tq, _ = q_ref.shape
    tk = k_ref.shape[0]
    j = jj if n_band == 0 else i - (n_band - 1) + jj
    active = (j <= i) if n_band == 0 else (j >= 0)

    @pl.when(jj == 0)
    def _():
        m_scr[...] = jnp.full_like(m_scr, NEG_BIG)
        l_scr[...] = jnp.zeros_like(l_scr)
        acc_scr[...] = jnp.zeros_like(acc_scr)

    def step(positional_mask):
        k = k_ref[...]
        v = v_ref[...]
        t_pos = i * tq + lax.broadcasted_iota(jnp.int32, (tq, 1), 0)
        k_pos = j * tk + lax.broadcasted_iota(jnp.int32, (1, tk), 1)
        d = t_pos - k_pos
        q = q_ref[...].reshape(n_heads * tq, q_ref.shape[2])
        s = lax.dot_general(q, k, _NT, preferred_element_type=F32).reshape(n_heads, tq, tk)
        s = s - slope_ref[...] * d.astype(F32)[None]
        if positional_mask:
            in_range = (d >= 0) if n_band == 0 else ((d >= 0) & (d < WINDOW))
            s = jnp.where(in_range[None], s, MASK_FILL)
        m_old = m_scr[...]
        m_new = jnp.maximum(m_old, jnp.max(s, axis=-1, keepdims=True))
        e = jnp.exp(s - m_new)
        alpha = jnp.exp(m_old - m_new)
        l_scr[...] = alpha * l_scr[...] + jnp.sum(e, axis=-1, keepdims=True)
        pv = jnp.dot(e.astype(BF16).reshape(n_heads * tq, tk), v, preferred_element_type=F32)
        acc_scr[...] = alpha * acc_scr[...] + pv.reshape(acc_scr.shape)
        m_scr[...] = m_new

    if n_band == 0:
        @pl.when(j < i)
        def _():
            step(False)

        @pl.when(j == i)
        def _():
            step(True)
    else:
        @pl.when(active)
        def _():
            step(True)

    @pl.when(jj == pl.num_programs(3) - 1)
    def _():
        outs = [acc_scr[r] / jnp.maximum(l_scr[r], TINY) for r in range(n_heads)]
        o_ref[...] = jnp.concatenate(outs, axis=-1)


def nsa_flash_prompt(q, k, v, slopes, n_band, tile=ATT_TILE):
    b, g, r, t, dq = q.shape
    dh = v.shape[-1]
    nt = t // tile
    if n_band == 0:
        n_steps = nt
        kv_map = lambda b_, g_, i, jj: (b_, g_, jnp.minimum(i, jj), 0)
    else:
        n_steps = n_band
        kv_map = lambda b_, g_, i, jj: (b_, g_, jnp.maximum(i - (n_band - 1) + jj, 0), 0)
    return pl.pallas_call(
        functools.partial(_nsa_flash_kernel, n_band=n_band),
        grid=(b, g, nt, n_steps),
        in_specs=[
            pl.BlockSpec((None, None, r, tile, dq), lambda b_, g_, i, jj: (b_, g_, 0, i, 0)),
            pl.BlockSpec((None, None, tile, dq), kv_map),
            pl.BlockSpec((None, None, tile, dh), kv_map),
            pl.BlockSpec((None, r, 1, 1), lambda b_, g_, i, jj: (g_, 0, 0, 0)),
        ],
        out_specs=pl.BlockSpec((None, tile, r * dh), lambda b_, g_, i, jj: (b_, i, g_)),
        out_shape=jax.ShapeDtypeStruct((b, t, g * r * dh), F32),
        scratch_shapes=[pltpu.VMEM((r, tile, 1), F32), pltpu.VMEM((r, tile, 1), F32), pltpu.VMEM((r, tile, dh), F32)],
        compiler_params=_params("arbitrary", "arbitrary", "arbitrary", "arbitrary"),
        name="nsa_flash_sel" if n_band == 0 else "nsa_flash_win",
    )(q, k, v, slopes.reshape(g, r, 1, 1))


def _dot_hi_lo(y, w01):
    hi = y.astype(BF16)
    lo = (y - hi.astype(F32)).astype(BF16)
    return jnp.dot(hi, w01, preferred_element_type=F32) + jnp.dot(lo, w01, preferred_element_type=F32)


def _nsa_cmp_sample_kernel(pt_ref, q_ref, new_ref, w_ref, place_ref, slope_ref, a_ref, *rest, n_pages, n_new, past,
                           nsb):
    pages = rest[:n_pages]
    o_ref, sel_ref = rest[n_pages:n_pages + 2]
    f_scr, s_scr = rest[n_pages + 2:]
    cpp = PAGE_SIZE // CMP_STRIDE
    ppg = PAGE_SIZE // cpp
    nc = n_pages * cpp
    w1 = w_ref[0]
    w2 = w_ref[1]
    for g0 in range(0, n_pages, ppg):
        acc_f = jnp.zeros((NSA_KV_W, PAGE_SIZE), F32)
        acc_s = jnp.zeros((NSA_KV_W, PAGE_SIZE), F32)
        for pp in range(ppg):
            x = pages[g0 + pp][...]
            place = place_ref[pp]
            acc_f = acc_f + _dot_hi_lo(x * w1, place)
            acc_s = acc_s + _dot_hi_lo(x * w2, place)
        f_scr[:, g0 * cpp:g0 * cpp + PAGE_SIZE] = acc_f
        s_scr[:, g0 * cpp:g0 * cpp + PAGE_SIZE] = acc_s
    xn = new_ref[...]
    lane = lax.broadcasted_iota(jnp.int32, (1, PAGE_SIZE), 1)
    f_scr[:, nc:nc + PAGE_SIZE] = jnp.where(lane == 0, jnp.sum(xn * w1[:, :CMP_STRIDE], axis=1, keepdims=True), 0.0)
    s_scr[:, nc:nc + PAGE_SIZE] = jnp.where(lane == 0, jnp.sum(xn * w2[:, :CMP_STRIDE], axis=1, keepdims=True), 0.0)
    s_all = s_scr[...]
    kc = f_scr[:, 0:nc] + pltpu.roll(s_all, s_all.shape[1] - 1, 1)[:, 0:nc]
    kk = kc[:NSA_KV_HALF].astype(BF16)
    kv = kc[NSA_KV_HALF:].astype(BF16)
    q = q_ref[...]
    rows = q.shape[0]
    row = lax.broadcasted_iota(jnp.int32, (rows, 1), 0)
    q_pos = past + ((row // NSA_GROUP) & (n_new - 1))
    c_idx = lax.broadcasted_iota(jnp.int32, (1, nc), 1)
    dc = q_pos - (c_idx * CMP_STRIDE + (CMP_BLOCK - 1))
    valid = dc >= 0
    s = jnp.dot(q, kk, preferred_element_type=F32) - slope_ref[...] * dc.astype(F32)
    e, l = _masked_softmax_parts(s, valid)
    p = e / jnp.maximum(l, TINY)
    o_ref[...] = lax.dot_general(p.astype(BF16), kv, _NT, preferred_element_type=F32)
    imp = jnp.sum(p.reshape(rows // NSA_GROUP, NSA_GROUP, nc), axis=1)
    blk_imp = jnp.dot(imp, a_ref[...], preferred_element_type=F32, precision=lax.Precision.HIGHEST)
    row_g = lax.broadcasted_iota(jnp.int32, (rows // NSA_GROUP, 1), 0)
    score = _block_scores(blk_imp, past + (row_g & (n_new - 1)))
    blk = lax.broadcasted_iota(jnp.int32, score.shape, 1)
    score = jnp.where(blk < nsb, score, -jnp.inf)
    sel_ref[...] = _topk_mask(score, min(SEL_TOP, nsb))


def nsa_cmp_sample(page_table, q_bd, new_rows, w12, slope_rows, cache, layer, n_new):
    db, rows, _ = q_bd.shape
    n_pages = page_table.shape[1]
    past = n_pages * PAGE_SIZE
    cpp = PAGE_SIZE // CMP_STRIDE
    ppg = PAGE_SIZE // cpp
    nc = past // CMP_STRIDE
    nsb = -(-(past + n_new) // SEL_BLOCK)
    nsb_pad = -(-nsb // 128) * 128
    assert n_new <= CMP_STRIDE and new_rows.shape[2] == CMP_STRIDE and n_new & (n_new - 1) == 0
    assert n_pages % ppg == 0
    a_mat = _imp_to_block_matrix(nc, nsb_pad)
    tok = np.arange(PAGE_SIZE)[None, :, None]
    place = jnp.asarray(np.arange(PAGE_SIZE)[None, None, :] == np.arange(ppg)[:, None, None] * cpp + tok // CMP_STRIDE,
                        dtype=BF16)
    grid_spec = pltpu.PrefetchScalarGridSpec(
        num_scalar_prefetch=1,
        grid=(db,),
        in_specs=[
            pl.BlockSpec((None, rows, NSA_KV_HALF), lambda b, pt: (b, 0, 0)),
            pl.BlockSpec((None, NSA_KV_W, CMP_STRIDE), lambda b, pt: (b, 0, 0)),
            pl.BlockSpec((2, NSA_KV_W, PAGE_SIZE), lambda b, pt: (0, 0, 0)),
            pl.BlockSpec((ppg, PAGE_SIZE, PAGE_SIZE), lambda b, pt: (0, 0, 0)),
            pl.BlockSpec((rows, 1), lambda b, pt: (0, 0)),
            pl.BlockSpec((nc, nsb_pad), lambda b, pt: (0, 0)),
        ] + _page_specs(n_pages, NSA_KV_W, layer),
        out_specs=[
            pl.BlockSpec((None, rows, NSA_KV_HALF), lambda b, pt: (b, 0, 0)),
            pl.BlockSpec((None, rows // NSA_GROUP, nsb_pad), lambda b, pt: (b, 0, 0)),
        ],
        scratch_shapes=[pltpu.VMEM((NSA_KV_W, nc + PAGE_SIZE), F32), pltpu.VMEM((NSA_KV_W, nc + PAGE_SIZE), F32)],
    )
    return pl.pallas_call(
        functools.partial(_nsa_cmp_sample_kernel, n_pages=n_pages, n_new=n_new, past=past, nsb=nsb),
        grid_spec=grid_spec,
        out_shape=[jax.ShapeDtypeStruct((db, rows, NSA_KV_HALF), F32),
                   jax.ShapeDtypeStruct((db, rows // NSA_GROUP, nsb_pad), F32)],
        compiler_params=_params("arbitrary"),
        name="nsa_cmp_sample",
    )(page_table, q_bd, new_rows, w12, place, slope_rows, a_mat, *([cache] * n_pages))


def _nsa_sel_sample_kernel(pt_ref, q_ref, selnew_ref, kmask_ref, news_ref, win_ref, neww_ref, slope_ref, oc_ref,
                           gate_ref, *rest, n_pages, n_new, past):
    pages = rest[:n_pages]
    o_ref = rest[n_pages]
    s_scr = rest[n_pages + 1]
    q2 = q_ref[...]
    qb = q2[:, :NSA_KV_HALF]
    rows = q2.shape[0]
    row = lax.broadcasted_iota(jnp.int32, (rows, 1), 0)
    q_pos = past + ((row // NSA_GROUP) & (n_new - 1))
    slope = slope_ref[...]
    for p in range(n_pages):
        kp = jnp.concatenate([pages[p][:NSA_KV_HALF, :].astype(BF16),
                              kmask_ref[:, p * PAGE_SIZE:(p + 1) * PAGE_SIZE]], axis=0)
        s_scr[:, p * PAGE_SIZE:(p + 1) * PAGE_SIZE] = jnp.dot(q2, kp, preferred_element_type=F32)
    k_pos = lax.broadcasted_iota(jnp.int32, (1, n_pages * PAGE_SIZE), 1)
    d = q_pos - k_pos
    s = s_scr[...] - slope * d.astype(F32)
    valid = (d >= 0) & (s > 0.1 * NEG_BIG)
    s = jnp.where(valid, s, NEG_BIG)
    news = news_ref[...]
    n_pad = news.shape[0]
    dn = q_pos - (past + lax.broadcasted_iota(jnp.int32, (1, n_pad), 1))
    valid_n = (dn >= 0) & (selnew_ref[...] > 0.5)
    s_n = (lax.dot_general(qb, news[:, :NSA_KV_HALF].astype(BF16), _NT, preferred_element_type=F32)
           - slope * dn.astype(F32))
    s_n = jnp.where(valid_n, s_n, NEG_BIG)
    m = jnp.maximum(jnp.max(s, axis=-1, keepdims=True), jnp.max(s_n, axis=-1, keepdims=True))
    e = jnp.where(valid, jnp.exp(s - m), 0.0)
    e_n = jnp.where(valid_n, jnp.exp(s_n - m), 0.0)
    l = jnp.sum(e, axis=-1, keepdims=True) + jnp.sum(e_n, axis=-1, keepdims=True)
    eb = e.astype(BF16)
    acc = jnp.dot(e_n.astype(BF16), news[:, NSA_KV_HALF:].astype(BF16), preferred_element_type=F32)
    for p in range(n_pages):
        vp = pages[p][NSA_KV_HALF:, :].astype(BF16)
        acc = acc + lax.dot_general(eb[:, p * PAGE_SIZE:(p + 1) * PAGE_SIZE], vp, _NT, preferred_element_type=F32)
    o_s = acc / jnp.maximum(l, TINY)
    kw = jnp.concatenate([win_ref[...], neww_ref[...]], axis=0)
    n_win = win_ref.shape[0]
    kw_pos = past - n_win + lax.broadcasted_iota(jnp.int32, (1, kw.shape[0]), 1)
    dw = q_pos - kw_pos
    valid_w = (dw >= 0) & (dw < WINDOW) & (kw_pos >= 0)
    s_w = (lax.dot_general(qb, kw[:, :NSA_KV_HALF].astype(BF16), _NT, preferred_element_type=F32)
           - slope * dw.astype(F32))
    e_w, l_w = _masked_softmax_parts(s_w, valid_w)
    p_w = e_w / jnp.maximum(l_w, TINY)
    o_w = jnp.dot(p_w.astype(BF16), kw[:, NSA_KV_HALF:].astype(BF16), preferred_element_type=F32)
    gate = gate_ref[...]
    o = gate[:, 0:1] * oc_ref[...] + gate[:, 1:2] * o_s + gate[:, 2:3] * o_w
    o_ref[...] = jnp.where(row < rows // NSA_KV_HEADS, o[:, :NSA_HEAD_DIM], o[:, NSA_HEAD_DIM:])


def nsa_sel_sample(page_table, q2, selnew, kmask, news, win, neww, slope_rows, o_c, gates, cache, layer, n_new):
    db, rows, _ = q2.shape
    n_pages = page_table.shape[1]
    past = n_pages * PAGE_SIZE
    n_win = win.shape[1]
    n_pad = news.shape[1]
    per_seq = lambda *shape: pl.BlockSpec((None,) + shape, lambda b, pt: (b,) + (0,) * len(shape))
    grid_spec = pltpu.PrefetchScalarGridSpec(
        num_scalar_prefetch=1,
        grid=(db,),
        in_specs=[
            per_seq(rows, q2.shape[-1]),
            per_seq(rows, 1),
            pl.BlockSpec(kmask.shape, lambda b, pt: (0, 0)),
            per_seq(n_pad, NSA_KV_W),
            per_seq(n_win, NSA_KV_W),
            per_seq(n_pad, NSA_KV_W),
            pl.BlockSpec((rows, 1), lambda b, pt: (0, 0)),
            per_seq(rows, NSA_KV_HALF),
            per_seq(rows, 3),
        ] + _page_specs(n_pages, NSA_KV_W, layer),
        out_specs=per_seq(rows, NSA_HEAD_DIM),
        scratch_shapes=[pltpu.VMEM((rows, past), F32)],
    )
    return pl.pallas_call(
        functools.partial(_nsa_sel_sample_kernel, n_pages=n_pages, n_new=n_new, past=past),
        grid_spec=grid_spec,
        out_shape=jax.ShapeDtypeStruct((db, rows, NSA_HEAD_DIM), F32),
        compiler_params=_params("arbitrary"),
        name="nsa_sel_sample",
    )(page_table, q2, selnew, kmask, news, win, neww, slope_rows, o_c, gates, *([cache] * n_pages))


def even_project(proj, pos, g_q, w_q_up, g_kv, w_kv_up):
    sizes = [MLA_Q_LORA, MLA_KV_LORA, MLA_ROPE, NSA_HEADS * NSA_HEAD_DIM, NSA_KV_W, NSA_KV_W, NSA_KV_W, 3 * NSA_HEADS]
    cuts = [int(c) for c in np.cumsum(sizes)[:-1]]
    cq, ckv, kr, nq, kvc, kvs, kvw, gt = jnp.split(proj, cuts, axis=-1)
    q = jnp.einsum('mc,chd->mhd', rmsnorm(cq, g_q), w_q_up)
    q_abs = jnp.einsum('mhn,chn->mhc', q[..., :MLA_NOPE], w_kv_up[..., :MLA_NOPE]) * MLA_SCALE
    q_rope = rope_rows(q[..., MLA_NOPE:], pos) * MLA_SCALE
    q_cat = jnp.concatenate([q_abs, q_rope], axis=-1).astype(BF16)
    ckv_rows = jnp.concatenate([rmsnorm(ckv, g_kv), rope_rows(kr, pos)], axis=-1)
    nq = (nq.reshape(-1, NSA_KV_HEADS, NSA_GROUP, NSA_HEAD_DIM) * NSA_SCALE).astype(BF16)
    gates = jax.nn.sigmoid(gt.reshape(-1, NSA_KV_HEADS, NSA_GROUP, 3))
    return q_cat, ckv_rows, nq, kvc, kvs, kvw, gates


def chunk_partials(rows, w):
    B, L = rows.shape[:2]
    rows = pad_rows(rows, L + (-L) % CMP_STRIDE)
    ch = rows.reshape((B, rows.shape[1] // CMP_STRIDE, CMP_STRIDE) + rows.shape[2:])
    first = jnp.einsum('bcjegd,ejg->bcegd', ch, w[:, :CMP_STRIDE], precision=lax.Precision.HIGHEST)
    second = jnp.einsum('bcjegd,ejg->bcegd', ch, w[:, CMP_STRIDE:], precision=lax.Precision.HIGHEST)
    return first, second


def _cmp_weight_rows(w_cmp):
    w = jnp.transpose(w_cmp, (1, 0, 2))
    w = jnp.broadcast_to(w[..., None], w.shape + (NSA_HEAD_DIM,)).reshape(CMP_BLOCK, NSA_KV_W)
    return w.reshape(2, CMP_STRIDE, NSA_KV_W)


def even_mixer_prompt(q_cat, ckv_rows, nq, kvc, kvs, kvw, gates, w_up, w_cmp, B, T):
    slopes = alibi_slopes()
    o_m = mla_prompt(jnp.transpose(q_cat.reshape(B, T, MLA_HEADS, MLA_ROW), (0, 2, 1, 3)),
                     ckv_rows.reshape(B, T, MLA_ROW).astype(BF16), w_up)
    q = jnp.transpose(nq.reshape(B, T, NSA_KV_HEADS, NSA_GROUP, NSA_HEAD_DIM), (0, 2, 3, 1, 4))
    kv5 = (B, T, 2, NSA_KV_HEADS, NSA_HEAD_DIM)
    first, second = chunk_partials(kvc.reshape(kv5), w_cmp)
    kc = first[:, :-1] + second[:, 1:]
    n_cmp = kc.shape[1]
    n_pad = -(-n_cmp // 128) * 128
    kc = jnp.transpose(pad_rows(kc, n_pad), (2, 0, 3, 1, 4)).astype(BF16)
    o_c, sel = nsa_cmp_prompt(q, kc[0], kc[1], slopes, n_cmp)
    nsb = sel.shape[-1]
    q_sel = jnp.concatenate([q, jnp.broadcast_to((1.0 - sel).astype(BF16)[:, :, None], q.shape[:4] + (nsb,))], axis=-1)
    ks = jnp.transpose(kvs.reshape(kv5), (2, 0, 3, 1, 4)).astype(BF16)
    own_blk = (jnp.arange(T)[:, None] // SEL_BLOCK == jnp.arange(nsb)[None, :])
    k_mask = jnp.where(own_blk, MASK_FILL, 0.0).astype(BF16)
    k_sel = jnp.concatenate([ks[0], jnp.broadcast_to(k_mask, ks[0].shape[:3] + (nsb,))], axis=-1)
    o_s = nsa_flash_prompt(q_sel, k_sel, ks[1], slopes, n_band=0)
    kw = jnp.transpose(kvw.reshape(kv5), (2, 0, 3, 1, 4)).astype(BF16)
    o_w = nsa_flash_prompt(q, kw[0], kw[1], slopes, n_band=WINDOW // ATT_TILE + 1)
    gt = gates.reshape(B, T, NSA_HEADS, 3)

    def gated(o, k):
        return o.reshape(B, T, NSA_HEADS, NSA_HEAD_DIM) * gt[..., k:k + 1]

    o_n = (gated(o_c, 0) + gated(o_s, 1) + gated(o_w, 2)).reshape(B, T, NSA_HEADS * NSA_HEAD_DIM)
    return jnp.concatenate([o_m, o_n], axis=-1)


def even_mixer_sample(q_cat, ckv_rows, nq, kvc, kvs, kvw, gates, w_up, w_cmp, page_table,
                      c_mla, c_cmp, c_sel, c_win, layer, DB, DS):
    n_pages = page_table.shape[1]
    past = n_pages * PAGE_SIZE
    slopes = alibi_slopes()
    n_pad = CMP_STRIDE

    def pad_new(a):
        return pad_rows(a.reshape(DB, DS, -1), n_pad)

    q_m = jnp.transpose(q_cat.reshape(DB, DS, MLA_HEADS, MLA_ROW), (0, 2, 1, 3)).reshape(DB, MLA_HEADS * DS, MLA_ROW)
    o_m = mla_sample(page_table, q_m, pad_new(ckv_rows).astype(BF16), w_up, c_mla, layer)
    rows = NSA_KV_HEADS * DS * NSA_GROUP
    q = jnp.transpose(nq.reshape(DB, DS, NSA_KV_HEADS, NSA_GROUP, NSA_HEAD_DIM), (0, 2, 1, 3, 4))
    eye = jnp.eye(NSA_KV_HEADS, dtype=BF16)
    q_bd = (q[:, :, :, :, None, :] * eye[None, :, None, None, :, None]).reshape(DB, rows, NSA_KV_HALF)
    slope_rows = jnp.broadcast_to(slopes[:, None, :], (NSA_KV_HEADS, DS, NSA_GROUP)).reshape(rows, 1)
    w12 = jnp.tile(jnp.swapaxes(_cmp_weight_rows(w_cmp), 1, 2), (1, 1, PAGE_SIZE // CMP_STRIDE))
    o_c, sel = nsa_cmp_sample(page_table, q_bd, jnp.swapaxes(pad_new(kvc), 1, 2), w12, slope_rows, c_cmp, layer, DS)
    npb = past // SEL_BLOCK
    unsel = jnp.repeat((1.0 - sel[..., :npb]).astype(BF16), NSA_GROUP, axis=1)
    q2 = jnp.concatenate([q_bd, unsel], axis=-1)
    selnew = jnp.repeat(sel[..., npb:npb + 1], NSA_GROUP, axis=1)
    own_blk = (jnp.arange(npb)[:, None] == jnp.arange(past)[None, :] // SEL_BLOCK)
    kmask = jnp.where(own_blk, NEG_BIG, 0.0).astype(BF16)
    g_rows = jnp.transpose(gates.reshape(DB, DS, NSA_KV_HEADS, NSA_GROUP, 3), (0, 2, 1, 3, 4)).reshape(DB, rows, 3)
    o_n = nsa_sel_sample(page_table, q2, selnew, kmask, pad_new(kvs), c_win.reshape(DB, -1, NSA_KV_W),
                         pad_new(kvw), slope_rows, o_c, g_rows, c_sel, layer, DS)
    o_n = jnp.transpose(o_n.reshape(DB, NSA_KV_HEADS, DS, NSA_GROUP, NSA_HEAD_DIM), (0, 2, 1, 3, 4))
    return jnp.concatenate([o_m, o_n.reshape(DB, DS, NSA_HEADS * NSA_HEAD_DIM)], axis=-1)


DN_REP = DN_V_HEADS // DN_QK_HEADS
DN_QK_W = DN_QK_HEADS * DN_K_DIM
DN_V_W = DN_V_HEADS * DN_V_DIM
_BNN = (((2,), (1,)), ((0,), (0,)))
_BNT = (((2,), (2,)), ((0,), (0,)))
_BTN = (((1,), (1,)), ((0,), (0,)))


def _bdot(a, b, dims=_BNN):
    return lax.dot_general(a, b, dims, preferred_element_type=F32)


def _dot3(a, b):
    a_hi = a.astype(BF16)
    a_lo = (a - a_hi.astype(F32)).astype(BF16)
    b_hi = b.astype(BF16)
    b_lo = (b - b_hi.astype(F32)).astype(BF16)
    return _bdot(a_hi, b_hi) + (_bdot(a_hi, b_lo) + _bdot(a_lo, b_hi))


def _softplus(x):
    return jnp.maximum(x, 0.0) + jnp.log(1.0 + jnp.exp(-jnp.abs(x)))


def _silu(x):
    return x * jax.nn.sigmoid(x)


def _l2norm(x):
    return x * lax.rsqrt(jnp.sum(x * x, axis=-1, keepdims=True) + L2_EPS)


def _short_conv(prev, x, w):
    n = x.shape[0]
    xp = jnp.concatenate([prev, x], axis=0)
    acc = xp[8:8 + n] * w[DN_CONV - 1:DN_CONV]
    for j in range(DN_CONV - 1):
        off = 8 - (DN_CONV - 1) + j
        acc = acc + xp[off:off + n] * w[j:j + 1]
    return _silu(acc)


def _pad_conv_buf(buf):
    return jnp.concatenate([jnp.zeros((8 - buf.shape[0], buf.shape[1]), F32), buf], axis=0)


def _chunk_terms(q, k, v, kk, qk, g_row, beta_row):
    c = q.shape[1]
    ii = lax.broadcasted_iota(jnp.int32, (1, c, c), 1)
    jj = lax.broadcasted_iota(jnp.int32, (1, c, c), 2)
    eye = ii == jj
    causal = ii >= jj
    g_col = jnp.sum(jnp.where(eye, g_row, 0.0), axis=2, keepdims=True)
    beta_col = jnp.sum(jnp.where(eye, beta_row, 0.0), axis=2, keepdims=True)
    gc_col = jnp.sum(jnp.where(causal, g_row, 0.0), axis=2, keepdims=True)
    gc_row = jnp.sum(jnp.where(ii <= jj, g_col, 0.0), axis=1, keepdims=True)
    decay = jnp.exp(jnp.where(causal, gc_col - gc_row, -jnp.inf))
    n_mat = jnp.where(ii > jj, -(kk * beta_col * decay), 0.0)
    t_mat = jnp.where(eye, 1.0, n_mat)
    p = n_mat
    for _ in range(int(np.log2(c)) - 1):
        p = _dot3(p, p)
        t_mat = t_mat + _dot3(t_mat, p)
    e_col = jnp.exp(gc_col)
    sol = _dot3(t_mat, jnp.concatenate([v * beta_col, k * (beta_col * e_col)], axis=-1))
    gc_last = gc_col[:, c - 1:c]
    return (sol[..., :v.shape[-1]], sol[..., v.shape[-1]:], q * e_col, k * jnp.exp(gc_last - gc_col), qk * decay,
            jnp.exp(gc_last))


def _state_step(s, u, w, qd, kd, attn, gl):
    s_b = s.astype(BF16)
    v_new = u - _bdot(w.astype(BF16), s_b)
    v_b = v_new.astype(BF16)
    o = _bdot(qd.astype(BF16), s_b) + _bdot(attn.astype(BF16), v_b)
    s = s * gl + _bdot(kd.astype(BF16), v_b, _BTN)
    return o, s


def _gated_out(o, z, g_norm):
    ms = jnp.mean(o * o, axis=-1, keepdims=True)
    return o * lax.rsqrt(ms + NORM_EPS) * g_norm * _silu(z)


def _gdn_chunk_kernel(b_ref, a_ref, alog_ref, dtb_ref, xq_ref, hq_ref, xk_ref, hk_ref, xv_ref, hv_ref,
                      wq_ref, wk_ref, wv_ref, bq_ref, bk_ref, bv_ref,
                      u_ref, w_ref, qd_ref, kd_ref, at_ref, gl_ref, *, chunk):
    i = pl.program_id(2)
    first = i == 0
    rows = xq_ref.shape[0]

    def conv(x_ref, halo_ref, buf_ref, w_ref_):
        prev = jnp.where(first, _pad_conv_buf(buf_ref[...]), halo_ref[...])
        return _short_conv(prev, x_ref[...], w_ref_[...])

    q = _l2norm(conv(xq_ref, hq_ref, bq_ref, wq_ref)) * DN_K_DIM ** -0.5
    k = _l2norm(conv(xk_ref, hk_ref, bk_ref, wk_ref))
    v = conv(xv_ref, hv_ref, bv_ref, wv_ref)
    nc = rows // chunk
    q3 = q.reshape(nc, chunk, DN_K_DIM)
    k3 = k.reshape(nc, chunk, DN_K_DIM)
    kb = k3.astype(BF16)
    kk = _bdot(kb, kb, _BNT)
    qk = _bdot(q3.astype(BF16), kb, _BNT)
    rep = lambda a: jnp.concatenate([a] * DN_REP, axis=0)
    v3 = jnp.concatenate([v[:, j * DN_V_DIM:(j + 1) * DN_V_DIM].reshape(nc, chunk, DN_V_DIM)
                          for j in range(DN_REP)], axis=0)
    g_row = jnp.concatenate([-jnp.exp(alog_ref[j]) * _softplus(a_ref[j] + dtb_ref[j]) for j in range(DN_REP)], axis=0)
    beta_row = jnp.concatenate([jax.nn.sigmoid(b_ref[j]) for j in range(DN_REP)], axis=0)
    u, w, qd, kd, attn, gl = _chunk_terms(rep(q3), rep(k3), v3, rep(kk), rep(qk), g_row, beta_row)
    for j in range(DN_REP):
        part = lambda a: a[j * nc:(j + 1) * nc].reshape(rows, a.shape[-1])
        u_ref[j] = part(u)
        w_ref[j] = part(w).astype(BF16)
        qd_ref[j] = part(qd).astype(BF16)
        kd_ref[j] = part(kd).astype(BF16)
        at_ref[j] = part(attn).astype(BF16)
        gl_ref[j] = jnp.broadcast_to(gl[j * nc:(j + 1) * nc], gl_ref.shape[1:])


def gdn_chunk_prompt(proj, ba_rows, conv_buf, w_conv, a_log, dt_bias, B, T, row_block=512):
    c = DN_CHUNK
    n_ch = T // c
    row_block = min(row_block, T)
    nb = T // row_block
    nc = row_block // c
    qk_blk = DN_QK_W // DN_K_DIM
    rep_w = DN_REP * DN_V_DIM

    def halo(b_, i):
        return jnp.maximum((b_ * T + i * row_block) // 8 - 1, 0)

    def x_specs(width, col):
        return [pl.BlockSpec((row_block, width), lambda b_, h, i: (b_ * nb + i, col(h))),
                pl.BlockSpec((8, width), lambda b_, h, i: (halo(b_, i), col(h)))]

    gate_spec = lambda off: pl.BlockSpec((None, DN_REP, nc, 1, c), lambda b_, h, i: (b_, off + h, i, 0, 0))
    head_spec = pl.BlockSpec((DN_REP, 1, 1), lambda b_, h, i: (h, 0, 0))
    per_head = lambda width, dt: jax.ShapeDtypeStruct((B, DN_V_HEADS, T, width), dt)
    out_spec = lambda width: pl.BlockSpec((None, DN_REP, row_block, width), lambda b_, h, i: (b_, h, i, 0))
    v_col = lambda h: DN_CONV_DIM // rep_w - DN_V_W // rep_w + h
    return pl.pallas_call(
        functools.partial(_gdn_chunk_kernel, chunk=c),
        grid=(B, DN_QK_HEADS, nb),
        in_specs=[gate_spec(0), gate_spec(DN_V_HEADS // DN_REP), head_spec, head_spec]
        + x_specs(DN_K_DIM, lambda h: h) + x_specs(DN_K_DIM, lambda h: qk_blk + h) + x_specs(rep_w, v_col)
        + [pl.BlockSpec((DN_CONV, DN_K_DIM), lambda b_, h, i: (0, h)),
           pl.BlockSpec((DN_CONV, DN_K_DIM), lambda b_, h, i: (0, qk_blk + h)),
           pl.BlockSpec((DN_CONV, rep_w), lambda b_, h, i: (0, v_col(h))),
           pl.BlockSpec((None, DN_CONV - 1, DN_K_DIM), lambda b_, h, i: (b_, 0, h)),
           pl.BlockSpec((None, DN_CONV - 1, DN_K_DIM), lambda b_, h, i: (b_, 0, qk_blk + h)),
           pl.BlockSpec((None, DN_CONV - 1, rep_w), lambda b_, h, i: (b_, 0, v_col(h)))],
        out_specs=[out_spec(DN_V_DIM), out_spec(DN_K_DIM), out_spec(DN_K_DIM), out_spec(DN_K_DIM), out_spec(c),
                   pl.BlockSpec((None, DN_REP, nc, 1, 128), lambda b_, h, i: (b_, h, i, 0, 0))],
        out_shape=[per_head(DN_V_DIM, F32), per_head(DN_K_DIM, BF16), per_head(DN_K_DIM, BF16),
                   per_head(DN_K_DIM, BF16), per_head(c, BF16),
                   jax.ShapeDtypeStruct((B, DN_V_HEADS, n_ch, 1, 128), F32)],
        compiler_params=_params("arbitrary", "arbitrary", "arbitrary"),
        name="gdn_chunk_prompt",
    )(ba_rows, ba_rows, a_log.reshape(DN_V_HEADS, 1, 1), dt_bias.reshape(DN_V_HEADS, 1, 1),
      proj, proj, proj, proj, proj, proj, w_conv, w_conv, w_conv, conv_buf, conv_buf, conv_buf)


def _gdn_scan_kernel(s0_ref, u_ref, w_ref, qd_ref, kd_ref, at_ref, gl_ref, z_ref, gn_ref, o_ref, s_ref, *, chunk):
    n_heads = s0_ref.shape[0]
    n_ch = u_ref.shape[1] // chunk

    @pl.when(pl.program_id(2) == 0)
    def _():
        s_ref[...] = s0_ref[...]

    g_norm = gn_ref[...]
    for ci in range(n_ch):
        rows = slice(ci * chunk, (ci + 1) * chunk)
        o, s = _state_step(s_ref[...], u_ref[:, rows, :], w_ref[:, rows, :], qd_ref[:, rows, :], kd_ref[:, rows, :],
                           at_ref[:, rows, :], gl_ref[:, ci])
        s_ref[...] = s
        for j in range(n_heads):
            cols = slice(j * DN_V_DIM, (j + 1) * DN_V_DIM)
            o_ref[rows, cols] = _gated_out(o[j], z_ref[rows, cols], g_norm)


def gdn_scan_prompt(s0, terms, proj, g_norm, B, T, heads_per_step=8, row_block=512):
    u, w, qd, kd, attn, gl = terms
    hb = heads_per_step
    row_block = min(row_block, T)
    nb = T // row_block
    nc = row_block // DN_CHUNK
    z_col = DN_CONV_DIM // (hb * DN_V_DIM)
    per_head = lambda width: pl.BlockSpec((None, hb, row_block, width), lambda b_, h, i: (b_, h, i, 0))
    state_spec = pl.BlockSpec((None, hb, DN_K_DIM, DN_V_DIM), lambda b_, h, i: (b_, h, 0, 0))
    return pl.pallas_call(
        functools.partial(_gdn_scan_kernel, chunk=DN_CHUNK),
        grid=(B, DN_V_HEADS // hb, nb),
        in_specs=[state_spec, per_head(DN_V_DIM), per_head(DN_K_DIM), per_head(DN_K_DIM), per_head(DN_K_DIM),
                  per_head(DN_CHUNK), pl.BlockSpec((None, hb, nc, 1, 128), lambda b_, h, i: (b_, h, i, 0, 0)),
                  pl.BlockSpec((row_block, hb * DN_V_DIM), lambda b_, h, i: (b_ * nb + i, z_col + h)),
                  pl.BlockSpec((1, DN_V_DIM), lambda b_, h, i: (0, 0))],
        out_specs=[pl.BlockSpec((None, row_block, hb * DN_V_DIM), lambda b_, h, i: (b_, i, h)), state_spec],
        out_shape=[jax.ShapeDtypeStruct((B, T, DN_V_W), F32),
                   jax.ShapeDtypeStruct((B, DN_V_HEADS, DN_K_DIM, DN_V_DIM), F32)],
        compiler_params=_params("arbitrary", "arbitrary", "arbitrary"),
        name="gdn_scan_prompt",
    )(s0, u, w, qd, kd, attn, gl, proj, g_norm.reshape(1, DN_V_DIM))


def _gdn_sample_kernel(x_ref, buf_ref, wc_ref, bar_ref, alog_c_ref, dtb_c_ref, s0_ref, gn_ref, *rest):
    o_ref, s_ref = rest[-2:]
    x = x_ref[...]
    conv = _short_conv(_pad_conv_buf(buf_ref[...]), x[:, :DN_CONV_DIM], wc_ref[...])
    bar = bar_ref[...]
    g_row = -jnp.exp(alog_c_ref[...]) * _softplus(bar[DN_V_HEADS:] + dtb_c_ref[...])
    beta_row = jax.nn.sigmoid(bar[:DN_V_HEADS])
    g_norm = gn_ref[...]

    def heads(col0, n_heads, width, rep):
        return jnp.stack([conv[:, col0 + h * width:col0 + (h + 1) * width]
                          for h in range(n_heads) for _ in range(rep)], axis=0)

    q = _l2norm(heads(0, DN_QK_HEADS, DN_K_DIM, DN_REP)) * DN_K_DIM ** -0.5
    k = _l2norm(heads(DN_QK_W, DN_QK_HEADS, DN_K_DIM, DN_REP))
    v = heads(2 * DN_QK_W, DN_V_HEADS, DN_V_DIM, 1)
    kb = k.astype(BF16)
    u, w, qd, kd, attn, gl = _chunk_terms(q, k, v, _bdot(kb, kb, _BNT), _bdot(q.astype(BF16), kb, _BNT),
                                          g_row, beta_row)
    o, s = _state_step(s0_ref[...], u, w, qd, kd, attn, gl)
    s_ref[...] = s
    for hv in range(DN_V_HEADS):
        z = x[:, DN_CONV_DIM + hv * DN_V_DIM:DN_CONV_DIM + (hv + 1) * DN_V_DIM]
        o_ref[:, hv * DN_V_DIM:(hv + 1) * DN_V_DIM] = _gated_out(o[hv], z, g_norm)


def gdn_sample(proj, row0, ba_rows, conv_buf, s0_all, layer, s_stack, w_conv, a_log, dt_bias, g_norm, DB, DS):
    assert DS == 8 and row0 % DS == 0
    n_main = proj.shape[1]
    full = lambda shape: pl.BlockSpec(shape, lambda b: (0,) * len(shape))
    state_spec = pl.BlockSpec((None, None, DN_V_HEADS, DN_K_DIM, DN_V_DIM), lambda b: (layer, b, 0, 0, 0))
    in_specs = [pl.BlockSpec((DS, n_main), lambda b: (row0 // DS + b, 0)),
                pl.BlockSpec((None, DN_CONV - 1, DN_CONV_DIM), lambda b: (b, 0, 0)),
                full((DN_CONV, DN_CONV_DIM)),
                pl.BlockSpec((None, 2 * DN_V_HEADS, 1, DS), lambda b: (b, 0, 0, 0)),
                full((DN_V_HEADS, 1, 1)), full((DN_V_HEADS, 1, 1)),
                state_spec, full((1, DN_V_DIM))]
    args = [proj, conv_buf, w_conv, ba_rows, a_log.reshape(-1, 1, 1), dt_bias.reshape(-1, 1, 1), s0_all,
            g_norm.reshape(1, DN_V_DIM)]
    aliases = {}
    if s_stack is not None:
        in_specs.append(pl.BlockSpec(memory_space=pl.ANY))
        args.append(s_stack)
        aliases = {len(args) - 1: 1}
    return pl.pallas_call(
        _gdn_sample_kernel,
        grid=(DB,),
        in_specs=in_specs,
        out_specs=[pl.BlockSpec((None, DS, DN_V_W), lambda b: (b, 0, 0)), state_spec],
        out_shape=[jax.ShapeDtypeStruct((DB, DS, DN_V_W), F32), jax.ShapeDtypeStruct(s0_all.shape, F32)],
        input_output_aliases=aliases,
        compiler_params=_params("arbitrary"),
        name="gdn_sample",
    )(*args)


def delta_mixers(proj, proj_ba, conv_state, s0_sample, layer, s_stack, w_conv, a_log, dt_bias, g_norm, B, T, DB, DS):
    n_p = B * T
    assert T % DN_CHUNK == 0 and T >= DN_CONV - 1 and DS >= DN_CONV - 1
    n_ch = T // DN_CHUNK
    ba_p = jnp.transpose(proj_ba[:n_p].reshape(B, n_ch, DN_CHUNK, -1), (0, 3, 1, 2))[:, :, :, None, :]
    buf0 = jnp.zeros((B, DN_CONV - 1, DN_CONV_DIM), F32)
    s0 = jnp.zeros((B, DN_V_HEADS, DN_K_DIM, DN_V_DIM), F32)
    terms = gdn_chunk_prompt(proj, ba_p, buf0, w_conv, a_log, dt_bias, B, T)
    o_p, s_p = gdn_scan_prompt(s0, terms, proj, g_norm, B, T)
    c_p = jnp.stack([proj[(b + 1) * T - (DN_CONV - 1):(b + 1) * T, :DN_CONV_DIM] for b in range(B)])
    ba_s = jnp.transpose(proj_ba[n_p:].reshape(DB, DS, -1), (0, 2, 1))[:, :, None, :]
    o_s, s_stack = gdn_sample(proj, n_p, ba_s, conv_state, s0_sample, layer, s_stack, w_conv, a_log, dt_bias,
                              g_norm, DB, DS)
    x_s = proj[n_p:].reshape(DB, DS, -1)
    c_s = x_s[:, DS - (DN_CONV - 1):, :DN_CONV_DIM]
    return o_p, c_p, s_p, o_s, c_s, s_stack


def kernel(x_prompt, x_sample, cache_mla, cache_nsa_cmp, cache_nsa_sel, cache_nsa_win, state_delta,
           state_delta_conv, page_table, norm_g, w_ffn_in, w_ffn_out, w_in_even, g_q_lora, w_q_up,
           g_kv_lora, w_kv_up, w_cmp, w_o_even, w_in_odd, w_conv, a_log, dt_bias, g_dn_norm, w_out_odd):
    B, T, D = x_prompt.shape
    DB, DS, _ = x_sample.shape
    n_p = B * T
    n_s = DB * DS
    past = page_table.shape[1] * PAGE_SIZE
    x = jnp.concatenate([x_prompt.reshape(n_p, D), x_sample.reshape(n_s, D)], axis=0)
    pos = jnp.concatenate([jnp.tile(jnp.arange(T), B), jnp.tile(past + jnp.arange(DS), DB)])
    kv5 = (2, NSA_KV_HEADS, NSA_HEAD_DIM)
    c_mla = _pages_feature_major(cache_mla)
    c_cmp = _pages_feature_major(cache_nsa_cmp)
    c_sel = _pages_feature_major(cache_nsa_sel)

    mla_p, mla_s, cmp_p, cmp_s, sel_p, sel_s, win_p, win_s = [], [], [], [], [], [], [], []
    dst_p, dcv_p, dcv_s = [], [], []
    s_stack = None
    for l in range(DEPTH):
        ng = norm_g[l]
        i = l // 2
        x = ffn_half(x, ng[0], ng[1], w_ffn_in[l, 0].astype(BF16), w_ffn_out[l, 0].astype(BF16))
        if l % 2 == 0:
            n_in = w_in_even.shape[-1]
            proj = norm_proj(x, ng[2], w_in_even[i].astype(BF16), tm=256, tn=n_in)
            q_cat, ckv_rows, nq, kvc, kvs, kvw, gates = even_project(
                proj, pos, g_q_lora[i], w_q_up[i], g_kv_lora[i], w_kv_up[i])
            w_up = jnp.transpose(w_kv_up[i][..., MLA_NOPE:], (1, 0, 2)).astype(BF16)
            op = even_mixer_prompt(q_cat[:n_p], ckv_rows[:n_p], nq[:n_p], kvc[:n_p], kvs[:n_p], kvw[:n_p],
                                   gates[:n_p], w_up, w_cmp[i], B, T)
            os_ = even_mixer_sample(q_cat[n_p:], ckv_rows[n_p:], nq[n_p:], kvc[n_p:], kvs[n_p:], kvw[n_p:],
                                    gates[n_p:], w_up, w_cmp[i], page_table, c_mla, c_cmp, c_sel,
                                    cache_nsa_win[i], i, DB, DS)
            mla_p.append(ckv_rows[:n_p].reshape(B, T, MLA_ROW))
            mla_s.append(ckv_rows[n_p:].reshape(DB, DS, MLA_ROW))
            cmp_p.append(kvc[:n_p].reshape((B, T) + kv5)); cmp_s.append(kvc[n_p:].reshape((DB, DS) + kv5))
            sel_p.append(kvs[:n_p].reshape((B, T) + kv5)); sel_s.append(kvs[n_p:].reshape((DB, DS) + kv5))
            kvw_p = kvw[:n_p].reshape((B, T) + kv5)
            win_p.append(kvw_p[:, T - min(WINDOW, T):])
            win_s.append(jnp.concatenate([cache_nsa_win[i], kvw[n_p:].reshape((DB, DS) + kv5)], axis=1)[:, DS:])
            w_o = w_o_even[i]
        else:
            n_main = DN_CONV_DIM + DN_V_HEADS * DN_V_DIM
            w_in = w_in_odd[i]
            proj_main = norm_proj(x, ng[2], w_in[:, :n_main].astype(BF16), tm=512, tn=1024)
            proj_ba = norm_proj(x, ng[2], w_in[:, n_main:].astype(BF16), tm=512, tn=2 * DN_V_HEADS)
            op, c_p, S_p, os_, c_s, s_stack = delta_mixers(
                proj_main, proj_ba, state_delta_conv[i], state_delta, i, s_stack, w_conv[i], a_log[i], dt_bias[i],
                g_dn_norm[i], B, T, DB, DS)
            dst_p.append(S_p.astype(state_delta.dtype))
            dcv_p.append(c_p); dcv_s.append(c_s)
            w_o = w_out_odd[i]
        o = jnp.concatenate([op.reshape(n_p, -1), os_.reshape(n_s, -1)], axis=0)
        x = out_proj_residual(o, w_o.astype(BF16), x, ng[3])
        x = ffn_half(x, ng[4], ng[5], w_ffn_in[l, 1].astype(BF16), w_ffn_out[l, 1].astype(BF16))
    return (x[:n_p].reshape(B, T, D), x[n_p:].reshape(DB, DS, D),
            jnp.stack(mla_p), jnp.stack(mla_s),
            jnp.stack(cmp_p), jnp.stack(cmp_s),
            jnp.stack(sel_p), jnp.stack(sel_s),
            jnp.stack(win_p), jnp.stack(win_s),
            jnp.stack(dst_p), s_stack.astype(state_delta.dtype),
            jnp.stack(dcv_p), jnp.stack(dcv_s))
```

```python
import functools

import jax
import jax.numpy as jnp
import numpy as np
from jax import lax
from jax.experimental import pallas as pl
from jax.experimental.pallas import tpu as pltpu

D_MODEL = 2048
DEPTH = 4
PAGE_SIZE = 128
MLA_HEADS = 8
MLA_Q_LORA = 512
MLA_KV_LORA = 256
MLA_NOPE = 128
MLA_ROPE = 64
MLA_V = 128
MLA_ROW = MLA_KV_LORA + MLA_ROPE
ROPE_THETA = 10000.0
MLA_SCALE = (MLA_NOPE + MLA_ROPE) ** -0.5
NSA_HEADS = 16
NSA_KV_HEADS = 2
NSA_GROUP = NSA_HEADS // NSA_KV_HEADS
NSA_HEAD_DIM = 64
NSA_SCALE = NSA_HEAD_DIM ** -0.5
CMP_STRIDE = 16
CMP_BLOCK = 2 * CMP_STRIDE
SEL_BLOCK = 64
SEL_TOP = 16
SEL_RATIO = SEL_BLOCK // CMP_STRIDE
WINDOW = 512
FORCE_BONUS = 1.0e4
DN_QK_HEADS = 16
DN_V_HEADS = 32
DN_K_DIM = 128
DN_V_DIM = 128
DN_CONV = 4
DN_CHUNK = 64
DN_CONV_DIM = 2 * DN_QK_HEADS * DN_K_DIM + DN_V_HEADS * DN_V_DIM
D_FF = 5632
NORM_EPS = 1e-6
L2_EPS = 1e-6
NEG_BIG = -1e30
MASK_FILL = 2.0 * NEG_BIG
TINY = 1e-30
NSA_KV_W = 2 * NSA_KV_HEADS * NSA_HEAD_DIM
NSA_KV_HALF = NSA_KV_HEADS * NSA_HEAD_DIM

F32 = jnp.float32
BF16 = jnp.bfloat16

VMEM_LIMIT_BYTES = 56 * 1024 * 1024
ATT_TILE = 256

_NT = (((1,), (1,)), ((), ()))


def _params(*sem):
    return pltpu.CompilerParams(dimension_semantics=sem, vmem_limit_bytes=VMEM_LIMIT_BYTES)


def _ffn_kernel(x_ref, gpre_ref, gpost_ref, wg_ref, wu_ref, wo_ref, o_ref, h_scr, acc_scr):
    j = pl.program_id(1)

    @pl.when(j == 0)
    def _():
        x = x_ref[...]
        ms = jnp.mean(x * x, axis=-1, keepdims=True)
        h_scr[...] = (x * lax.rsqrt(ms + NORM_EPS) * gpre_ref[...]).astype(BF16)
        acc_scr[...] = jnp.zeros_like(acc_scr)

    h = h_scr[...]
    gate = jnp.dot(h, wg_ref[...], preferred_element_type=F32)
    up = jnp.dot(h, wu_ref[...], preferred_element_type=F32)
    a = (gate * jax.nn.sigmoid(gate) * up).astype(BF16)
    acc_scr[...] += jnp.dot(a, wo_ref[...], preferred_element_type=F32)

    @pl.when(j == pl.num_programs(1) - 1)
    def _():
        y = acc_scr[...]
        ms = jnp.mean(y * y, axis=-1, keepdims=True)
        o_ref[...] = x_ref[...] + 0.5 * (y * lax.rsqrt(ms + NORM_EPS) * gpost_ref[...])


def ffn_half(x, g_pre, g_post, w_in, w_out, tm=512, tf=512):
    m, d = x.shape
    f = w_out.shape[0]
    nf = f // tf
    assert m % tm == 0 and f % tf == 0
    return pl.pallas_call(
        _ffn_kernel,
        grid=(m // tm, nf),
        in_specs=[
            pl.BlockSpec((tm, d), lambda i, j: (i, 0)),
            pl.BlockSpec((1, d), lambda i, j: (0, 0)),
            pl.BlockSpec((1, d), lambda i, j: (0, 0)),
            pl.BlockSpec((d, tf), lambda i, j: (0, j)),
            pl.BlockSpec((d, tf), lambda i, j: (0, j + nf)),
            pl.BlockSpec((tf, d), lambda i, j: (j, 0)),
        ],
        out_specs=pl.BlockSpec((tm, d), lambda i, j: (i, 0)),
        out_shape=jax.ShapeDtypeStruct((m, d), F32),
        scratch_shapes=[pltpu.VMEM((tm, d), BF16), pltpu.VMEM((tm, d), F32)],
        compiler_params=_params("arbitrary", "arbitrary"),
        name="ffn_half",
    )(x, g_pre.reshape(1, d), g_post.reshape(1, d), w_in, w_in, w_out)


def _norm_proj_kernel(x_ref, g_ref, w_ref, o_ref, h_scr):
    @pl.when(pl.program_id(1) == 0)
    def _():
        x = x_ref[...]
        ms = jnp.mean(x * x, axis=-1, keepdims=True)
        h_scr[...] = (x * lax.rsqrt(ms + NORM_EPS) * g_ref[...]).astype(BF16)

    o_ref[...] = jnp.dot(h_scr[...], w_ref[...], preferred_element_type=F32)


def norm_proj(x, g, w, tm, tn):
    m, d = x.shape
    n = w.shape[1]
    assert m % tm == 0 and n % tn == 0
    return pl.pallas_call(
        _norm_proj_kernel,
        grid=(m // tm, n // tn),
        in_specs=[
            pl.BlockSpec((tm, d), lambda i, j: (i, 0)),
            pl.BlockSpec((1, d), lambda i, j: (0, 0)),
            pl.BlockSpec((d, tn), lambda i, j: (0, j)),
        ],
        out_specs=pl.BlockSpec((tm, tn), lambda i, j: (i, j)),
        out_shape=jax.ShapeDtypeStruct((m, n), F32),
        scratch_shapes=[pltpu.VMEM((tm, d), BF16)],
        compiler_params=_params("arbitrary", "arbitrary"),
        name="norm_proj",
    )(x, g.reshape(1, d), w)


def _out_proj_kernel(a_ref, w_ref, x_ref, g_ref, o_ref):
    y = jnp.dot(a_ref[...].astype(BF16), w_ref[...], preferred_element_type=F32)
    ms = jnp.mean(y * y, axis=-1, keepdims=True)
    o_ref[...] = x_ref[...] + y * lax.rsqrt(ms + NORM_EPS) * g_ref[...]


def out_proj_residual(a, w, x, g, tm=256):
    m, k = a.shape
    d = w.shape[1]
    assert m % tm == 0
    return pl.pallas_call(
        _out_proj_kernel,
        grid=(m // tm,),
        in_specs=[
            pl.BlockSpec((tm, k), lambda i: (i, 0)),
            pl.BlockSpec((k, d), lambda i: (0, 0)),
            pl.BlockSpec((tm, d), lambda i: (i, 0)),
            pl.BlockSpec((1, d), lambda i: (0, 0)),
        ],
        out_specs=pl.BlockSpec((tm, d), lambda i: (i, 0)),
        out_shape=jax.ShapeDtypeStruct((m, d), F32),
        compiler_params=_params("arbitrary"),
        name="out_proj_residual",
    )(a, w, x, g.reshape(1, d))


def rmsnorm(x, g):
    xf = x.astype(F32)
    y = xf * lax.rsqrt(jnp.mean(xf * xf, axis=-1, keepdims=True) + NORM_EPS)
    return (y * g.astype(F32)).astype(x.dtype)


def l2norm(x):
    xf = x.astype(F32)
    return xf * lax.rsqrt(jnp.sum(xf * xf, axis=-1, keepdims=True) + L2_EPS)


def rope_rows(x, pos):
    half = x.shape[-1] // 2
    inv = ROPE_THETA ** (-2.0 * jnp.arange(half, dtype=F32) / x.shape[-1])
    ang = pos.astype(F32)[:, None] * inv
    shape = (pos.shape[0],) + (1,) * (x.ndim - 2) + (half,)
    cos = jnp.cos(ang).reshape(shape)
    sin = jnp.sin(ang).reshape(shape)
    x1 = x[..., :half]
    x2 = x[..., half:]
    return jnp.concatenate([x1 * cos - x2 * sin, x1 * sin + x2 * cos], axis=-1)


def alibi_slopes():
    h = jnp.arange(NSA_HEADS, dtype=F32) + 1.0
    return (2.0 ** (-8.0 * h / NSA_HEADS)).reshape(NSA_KV_HEADS, NSA_GROUP)


def pad_rows(x, length):
    return jnp.pad(x, [(0, 0), (0, length - x.shape[1])] + [(0, 0)] * (x.ndim - 2))


def _masked_softmax_parts(s, valid):
    s = jnp.where(valid, s, NEG_BIG)
    m = jnp.max(s, axis=-1, keepdims=True)
    e = jnp.where(valid, jnp.exp(s - m), 0.0)
    return e, jnp.sum(e, axis=-1, keepdims=True)


def _topk_mask(score, k):
    n = score.shape[-1]
    lane = lax.broadcasted_iota(jnp.int32, score.shape, score.ndim - 1)
    sel = jnp.zeros(score.shape, F32)
    x = score
    for _ in range(k):
        m = jnp.max(x, axis=-1, keepdims=True)
        idx = jnp.min(jnp.where(x == m, lane, n), axis=-1, keepdims=True)
        hit = lane == idx
        sel = jnp.where(hit & (m > -jnp.inf), 1.0, sel)
        x = jnp.where(hit, -jnp.inf, x)
    return sel


def _block_scores(blk_imp, q_pos):
    blk = lax.broadcasted_iota(jnp.int32, blk_imp.shape, 1)
    cur = q_pos // SEL_BLOCK
    valid = blk <= cur
    forced = valid & ((blk == 0) | (blk >= cur - 1))
    return jnp.where(valid, blk_imp + FORCE_BONUS * forced.astype(F32), -jnp.inf)


def _imp_to_block_matrix(n_cmp, n_blk):
    c = np.arange(n_cmp)[:, None]
    n = np.arange(n_blk)[None, :]
    return jnp.asarray(((c // SEL_RATIO == n) | (c == SEL_RATIO * n - 1)).astype(np.float32))


def _mla_prompt_kernel(qt_ref, k_ref, vt_ref, wupt_ref, o_ref, m_scr, l_scr, acc_scr, *, tq):
    i = pl.program_id(1)
    j = pl.program_id(2)
    tk = k_ref.shape[0]
    cols = qt_ref.shape[1]
    heads = cols // tq

    @pl.when(j == 0)
    def _():
        m_scr[...] = jnp.full_like(m_scr, NEG_BIG)
        l_scr[...] = jnp.zeros_like(l_scr)
        acc_scr[...] = jnp.zeros_like(acc_scr)

    def step(diagonal):
        s = jnp.dot(k_ref[...], qt_ref[...], preferred_element_type=F32)
        if diagonal:
            col = lax.broadcasted_iota(jnp.int32, (1, cols), 1)
            t_pos = i * tq + (col & (tq - 1))
            k_pos = j * tk + lax.broadcasted_iota(jnp.int32, (tk, 1), 0)
            s = jnp.where(k_pos <= t_pos, s, MASK_FILL)
        m_old = m_scr[...]
        m_new = jnp.maximum(m_old, jnp.max(s, axis=0, keepdims=True))
        e = jnp.exp(s - m_new)
        alpha = jnp.exp(m_old - m_new)
        l_scr[...] = alpha * l_scr[...] + jnp.sum(e, axis=0, keepdims=True)
        acc_scr[...] = alpha * acc_scr[...] + jnp.dot(vt_ref[...], e.astype(BF16), preferred_element_type=F32)
        m_scr[...] = m_new

    @pl.when(j < i)
    def _():
        step(False)

    @pl.when(j == i)
    def _():
        step(True)
        o_lat = (acc_scr[...] / jnp.maximum(l_scr[...], TINY)).astype(BF16)
        for h in range(heads):
            o_ref[h * MLA_V:(h + 1) * MLA_V, :] = jnp.dot(wupt_ref[h], o_lat[:, h * tq:(h + 1) * tq],
                                                          preferred_element_type=F32)


def mla_prompt(q_cat, ckv, w_up, tile=ATT_TILE):
    b, h, t, dq = q_cat.shape
    assert t % tile == 0 and tile & (tile - 1) == 0
    nt = t // tile
    qt = jnp.transpose(q_cat.reshape(b, h, nt, tile, dq), (0, 2, 4, 1, 3)).reshape(b, nt, dq, h * tile)
    vt = jnp.swapaxes(ckv[..., :MLA_KV_LORA], 1, 2)
    o_t = pl.pallas_call(
        functools.partial(_mla_prompt_kernel, tq=tile),
        grid=(b, nt, nt),
        in_specs=[
            pl.BlockSpec((None, None, dq, h * tile), lambda b_, i, j: (b_, i, 0, 0)),
            pl.BlockSpec((None, tile, dq), lambda b_, i, j: (b_, jnp.minimum(i, j), 0)),
            pl.BlockSpec((None, MLA_KV_LORA, tile), lambda b_, i, j: (b_, 0, jnp.minimum(i, j))),
            pl.BlockSpec((h, MLA_V, MLA_KV_LORA), lambda b_, i, j: (0, 0, 0)),
        ],
        out_specs=pl.BlockSpec((None, h * MLA_V, tile), lambda b_, i, j: (b_, 0, i)),
        out_shape=jax.ShapeDtypeStruct((b, h * MLA_V, t), F32),
        scratch_shapes=[pltpu.VMEM((1, h * tile), F32), pltpu.VMEM((1, h * tile), F32),
                        pltpu.VMEM((MLA_KV_LORA, h * tile), F32)],
        compiler_params=_params("arbitrary", "arbitrary", "arbitrary"),
        name="mla_prompt",
    )(qt, ckv, vt, jnp.swapaxes(w_up, 1, 2))
    return jnp.swapaxes(o_t, 1, 2)


def _page_specs(n_pages, width, layer):
    return [pl.BlockSpec((None, None, width, PAGE_SIZE), lambda b, pt, p=p: (layer, pt[b, p], 0, 0))
            for p in range(n_pages)]


def _pages_feature_major(cache):
    c = cache.reshape(cache.shape[:3] + (-1,))
    return jnp.swapaxes(c, 2, 3)


def _mla_sample_kernel(pt_ref, q_ref, new_ref, wup_ref, *rest, n_pages, n_new):
    pages = rest[:n_pages]
    o_ref = rest[n_pages]
    s_scr = rest[n_pages + 1]
    q = q_ref[...]
    rows = q.shape[0]
    heads = rows // n_new
    for p in range(n_pages):
        kp = pages[p][...].astype(BF16)
        s_scr[:, p * PAGE_SIZE:(p + 1) * PAGE_SIZE] = jnp.dot(q, kp, preferred_element_type=F32)
    new = new_ref[...]
    s_new = lax.dot_general(q, new, _NT, preferred_element_type=F32)
    t_row = lax.broadcasted_iota(jnp.int32, (rows, 1), 0) & (n_new - 1)
    valid_new = lax.broadcasted_iota(jnp.int32, (1, new.shape[0]), 1) <= t_row
    s_new = jnp.where(valid_new, s_new, NEG_BIG)
    s = s_scr[...]
    m = jnp.maximum(jnp.max(s, axis=-1, keepdims=True), jnp.max(s_new, axis=-1, keepdims=True))
    e = jnp.exp(s - m).astype(BF16)
    e_new = jnp.where(valid_new, jnp.exp(s_new - m), 0.0)
    l = jnp.sum(e.astype(F32), axis=-1, keepdims=True) + jnp.sum(e_new, axis=-1, keepdims=True)
    acc = jnp.dot(e_new.astype(BF16), new[:, :MLA_KV_LORA], preferred_element_type=F32)
    for p in range(n_pages):
        vp = pages[p][:MLA_KV_LORA, :].astype(BF16)
        acc = acc + lax.dot_general(e[:, p * PAGE_SIZE:(p + 1) * PAGE_SIZE], vp, _NT, preferred_element_type=F32)
    o_lat = (acc / jnp.maximum(l, TINY)).astype(BF16)
    outs = [jnp.dot(o_lat[h * n_new:(h + 1) * n_new], wup_ref[h], preferred_element_type=F32) for h in range(heads)]
    o_ref[...] = jnp.concatenate(outs, axis=-1)


def mla_sample(page_table, q_cat, new_rows, w_up, cache, layer):
    db, rows, dq = q_cat.shape
    n_pages = page_table.shape[1]
    heads = w_up.shape[0]
    n_new = rows // heads
    assert n_new & (n_new - 1) == 0 and n_new <= new_rows.shape[1]
    n_pad = new_rows.shape[1]
    grid_spec = pltpu.PrefetchScalarGridSpec(
        num_scalar_prefetch=1,
        grid=(db,),
        in_specs=[
            pl.BlockSpec((None, rows, dq), lambda b, pt: (b, 0, 0)),
            pl.BlockSpec((None, n_pad, dq), lambda b, pt: (b, 0, 0)),
            pl.BlockSpec((heads, MLA_KV_LORA, MLA_V), lambda b, pt: (0, 0, 0)),
        ] + _page_specs(n_pages, dq, layer),
        out_specs=pl.BlockSpec((None, n_new, heads * MLA_V), lambda b, pt: (b, 0, 0)),
        scratch_shapes=[pltpu.VMEM((rows, n_pages * PAGE_SIZE), F32)],
    )
    return pl.pallas_call(
        functools.partial(_mla_sample_kernel, n_pages=n_pages, n_new=n_new),
        grid_spec=grid_spec,
        out_shape=jax.ShapeDtypeStruct((db, n_new, heads * MLA_V), F32),
        compiler_params=_params("arbitrary"),
        name="mla_sample",
    )(page_table, q_cat, new_rows, w_up, *([cache] * n_pages))


def _nsa_cmp_prompt_kernel(q_ref, kk_ref, kv_ref, slope_ref, a_ref, o_ref, sel_ref, *, n_cmp):
    i = pl.program_id(2)
    n_heads, tq, _ = q_ref.shape
    kk = kk_ref[...]
    kv = kv_ref[...]
    n_pad = kk.shape[0]
    t_pos = i * tq + lax.broadcasted_iota(jnp.int32, (tq, 1), 0)
    c_idx = lax.broadcasted_iota(jnp.int32, (1, n_pad), 1)
    dc = t_pos - (c_idx * CMP_STRIDE + (CMP_BLOCK - 1))
    valid = (dc >= 0) & (c_idx < n_cmp)
    dcf = dc.astype(F32)
    imp = jnp.zeros((tq, n_pad), F32)
    outs = []
    for r in range(n_heads):
        s = lax.dot_general(q_ref[r], kk, _NT, preferred_element_type=F32) - slope_ref[r] * dcf
        e, l = _masked_softmax_parts(s, valid)
        p = e / jnp.maximum(l, TINY)
        imp = imp + p
        outs.append(jnp.dot(p.astype(BF16), kv, preferred_element_type=F32))
    o_ref[...] = jnp.concatenate(outs, axis=-1)
    blk_imp = jnp.dot(imp, a_ref[...], preferred_element_type=F32, precision=lax.Precision.HIGHEST)
    sel_ref[...] = _topk_mask(_block_scores(blk_imp, t_pos), min(SEL_TOP, blk_imp.shape[-1]))


def nsa_cmp_prompt(q, kc_k, kc_v, slopes, n_cmp, tile=ATT_TILE):
    b, g, r, t, dh = q.shape
    n_pad = kc_k.shape[2]
    nsb = -(-t // SEL_BLOCK)
    a_mat = _imp_to_block_matrix(n_pad, nsb)
    return pl.pallas_call(
        functools.partial(_nsa_cmp_prompt_kernel, n_cmp=n_cmp),
        grid=(b, g, t // tile),
        in_specs=[
            pl.BlockSpec((None, None, r, tile, dh), lambda b_, g_, i: (b_, g_, 0, i, 0)),
            pl.BlockSpec((None, None, n_pad, dh), lambda b_, g_, i: (b_, g_, 0, 0)),
            pl.BlockSpec((None, None, n_pad, dh), lambda b_, g_, i: (b_, g_, 0, 0)),
            pl.BlockSpec((None, r, 1, 1), lambda b_, g_, i: (g_, 0, 0, 0)),
            pl.BlockSpec((n_pad, nsb), lambda b_, g_, i: (0, 0)),
        ],
        out_specs=[
            pl.BlockSpec((None, tile, r * dh), lambda b_, g_, i: (b_, i, g_)),
            pl.BlockSpec((None, None, tile, nsb), lambda b_, g_, i: (b_, g_, i, 0)),
        ],
        out_shape=[jax.ShapeDtypeStruct((b, t, g * r * dh), F32), jax.ShapeDtypeStruct((b, g, t, nsb), F32)],
        compiler_params=_params("arbitrary", "arbitrary", "arbitrary"),
        name="nsa_cmp_prompt",
    )(q, kc_k, kc_v, slopes.reshape(g, r, 1, 1), a_mat)


def _nsa_flash_kernel(qt_ref, k_ref, vt_ref, slope_ref, o_ref, m_scr, l_scr, acc_scr, *, n_band, tq):
    i = pl.program_id(2)
    jj = pl.program_id(3)
    tk = k_ref.shape[0]
    n_heads = qt_ref.shape[1] // tq
    j = jj if n_band == 0 else i - (n_band - 1) + jj

    @pl.when(jj == 0)
    def _():
        m_scr[...] = jnp.full_like(m_scr, NEG_BIG)
        l_scr[...] = jnp.zeros_like(l_scr)
        acc_scr[...] = jnp.zeros_like(acc_scr)

    def step(positional_mask):
        k_pos = j * tk + lax.broadcasted_iota(jnp.int32, (tk, 1), 0)
        t_pos = i * tq + lax.broadcasted_iota(jnp.int32, (1, tq), 1)
        d = t_pos - k_pos
        d_all = jnp.concatenate([d.astype(F32)] * n_heads, axis=1)
        s = jnp.dot(k_ref[...], qt_ref[...], preferred_element_type=F32) - slope_ref[...] * d_all
        if positional_mask:
            in_range = (d_all >= 0.0) if n_band == 0 else ((d_all >= 0.0) & (d_all < float(WINDOW)))
            s = jnp.where(in_range, s, MASK_FILL)
        m_old = m_scr[...]
        m_new = jnp.maximum(m_old, jnp.max(s, axis=0, keepdims=True))
        e = jnp.exp(s - m_new)
        alpha = jnp.exp(m_old - m_new)
        l_scr[...] = alpha * l_scr[...] + jnp.sum(e, axis=0, keepdims=True)
        acc_scr[...] = alpha * acc_scr[...] + jnp.dot(vt_ref[...], e.astype(BF16), preferred_element_type=F32)
        m_scr[...] = m_new

    if n_band == 0:
        @pl.when(j < i)
        def _():
            step(False)

        @pl.when(j == i)
        def _():
            step(True)
    else:
        @pl.when(j >= 0)
        def _():
            step(True)

    @pl.when(jj == pl.num_programs(3) - 1)
    def _():
        o_ref[...] = acc_scr[...] / jnp.maximum(l_scr[...], TINY)


def nsa_flash_prompt(q, k, v, slopes, n_band, tile=ATT_TILE):
    b, g, r, t, dq = q.shape
    dh = v.shape[-1]
    nt = t // tile
    qt = jnp.transpose(q.reshape(b, g, r, nt, tile, dq), (0, 1, 3, 5, 2, 4)).reshape(b, g, nt, dq, r * tile)
    vt = jnp.swapaxes(v, 2, 3)
    slope_cols = jnp.repeat(slopes, tile, axis=1).reshape(g, 1, r * tile)
    if n_band == 0:
        n_steps = nt
        kv_idx = lambda i, jj: jnp.minimum(i, jj)
    else:
        n_steps = n_band
        kv_idx = lambda i, jj: jnp.maximum(i - (n_band - 1) + jj, 0)
    o_t = pl.pallas_call(
        functools.partial(_nsa_flash_kernel, n_band=n_band, tq=tile),
        grid=(b, g, nt, n_steps),
        in_specs=[
            pl.BlockSpec((None, None, None, dq, r * tile), lambda b_, g_, i, jj: (b_, g_, i, 0, 0)),
            pl.BlockSpec((None, None, tile, dq), lambda b_, g_, i, jj: (b_, g_, kv_idx(i, jj), 0)),
            pl.BlockSpec((None, None, dh, tile), lambda b_, g_, i, jj: (b_, g_, 0, kv_idx(i, jj))),
            pl.BlockSpec((None, 1, r * tile), lambda b_, g_, i, jj: (g_, 0, 0)),
        ],
        out_specs=pl.BlockSpec((None, None, None, dh, r * tile), lambda b_, g_, i, jj: (b_, g_, i, 0, 0)),
        out_shape=jax.ShapeDtypeStruct((b, g, nt, dh, r * tile), F32),
        scratch_shapes=[pltpu.VMEM((1, r * tile), F32), pltpu.VMEM((1, r * tile), F32),
                        pltpu.VMEM((dh, r * tile), F32)],
        compiler_params=_params("arbitrary", "arbitrary", "arbitrary", "arbitrary"),
        name="nsa_flash_sel" if n_band == 0 else "nsa_flash_win",
    )(qt, k, vt, slope_cols)
    o = jnp.transpose(o_t.reshape(b, g, nt, dh, r, tile), (0, 2, 5, 1, 4, 3))
    return o.reshape(b, t, g * r * dh)


def _dot_hi_lo(y, w01):
    hi = y.astype(BF16)
    lo = (y - hi.astype(F32)).astype(BF16)
    return jnp.dot(hi, w01, preferred_element_type=F32) + jnp.dot(lo, w01, preferred_element_type=F32)


def _nsa_cmp_sample_kernel(pt_ref, q_ref, new_ref, w_ref, place_ref, slope_ref, a_ref, *rest, n_pages, n_new, past,
                           nsb):
    pages = rest[:n_pages]
    o_ref, sel_ref = rest[n_pages:n_pages + 2]
    f_scr, s_scr = rest[n_pages + 2:]
    cpp = PAGE_SIZE // CMP_STRIDE
    ppg = PAGE_SIZE // cpp
    nc = n_pages * cpp
    w1 = w_ref[0]
    w2 = w_ref[1]
    for g0 in range(0, n_pages, ppg):
        acc_f = jnp.zeros((NSA_KV_W, PAGE_SIZE), F32)
        acc_s = jnp.zeros((NSA_KV_W, PAGE_SIZE), F32)
        for pp in range(ppg):
            x = pages[g0 + pp][...]
            place = place_ref[pp]
            acc_f = acc_f + _dot_hi_lo(x * w1, place)
            acc_s = acc_s + _dot_hi_lo(x * w2, place)
        f_scr[:, g0 * cpp:g0 * cpp + PAGE_SIZE] = acc_f
        s_scr[:, g0 * cpp:g0 * cpp + PAGE_SIZE] = acc_s
    xn = new_ref[...]
    lane = lax.broadcasted_iota(jnp.int32, (1, PAGE_SIZE), 1)
    f_scr[:, nc:nc + PAGE_SIZE] = jnp.where(lane == 0, jnp.sum(xn * w1[:, :CMP_STRIDE], axis=1, keepdims=True), 0.0)
    s_scr[:, nc:nc + PAGE_SIZE] = jnp.where(lane == 0, jnp.sum(xn * w2[:, :CMP_STRIDE], axis=1, keepdims=True), 0.0)
    s_all = s_scr[...]
    kc = f_scr[:, 0:nc] + pltpu.roll(s_all, s_all.shape[1] - 1, 1)[:, 0:nc]
    kk = kc[:NSA_KV_HALF].astype(BF16)
    kv = kc[NSA_KV_HALF:].astype(BF16)
    q = q_ref[...]
    rows = q.shape[0]
    row = lax.broadcasted_iota(jnp.int32, (rows, 1), 0)
    q_pos = past + ((row // NSA_GROUP) & (n_new - 1))
    c_idx = lax.broadcasted_iota(jnp.int32, (1, nc), 1)
    dc = q_pos - (c_idx * CMP_STRIDE + (CMP_BLOCK - 1))
    valid = dc >= 0
    s = jnp.dot(q, kk, preferred_element_type=F32) - slope_ref[...] * dc.astype(F32)
    e, l = _masked_softmax_parts(s, valid)
    p = e / jnp.maximum(l, TINY)
    o_ref[...] = lax.dot_general(p.astype(BF16), kv, _NT, preferred_element_type=F32)
    imp = jnp.sum(p.reshape(rows // NSA_GROUP, NSA_GROUP, nc), axis=1)
    blk_imp = jnp.dot(imp, a_ref[...], preferred_element_type=F32, precision=lax.Precision.HIGHEST)
    row_g = lax.broadcasted_iota(jnp.int32, (rows // NSA_GROUP, 1), 0)
    score = _block_scores(blk_imp, past + (row_g & (n_new - 1)))
    blk = lax.broadcasted_iota(jnp.int32, score.shape, 1)
    score = jnp.where(blk < nsb, score, -jnp.inf)
    sel_ref[...] = _topk_mask(score, min(SEL_TOP, nsb))


def nsa_cmp_sample(page_table, q_bd, new_rows, w12, slope_rows, cache, layer, n_new):
    db, rows, _ = q_bd.shape
    n_pages = page_table.shape[1]
    past = n_pages * PAGE_SIZE
    cpp = PAGE_SIZE // CMP_STRIDE
    ppg = PAGE_SIZE // cpp
    nc = past // CMP_STRIDE
    nsb = -(-(past + n_new) // SEL_BLOCK)
    nsb_pad = -(-nsb // 128) * 128
    assert n_new <= CMP_STRIDE and new_rows.shape[2] == CMP_STRIDE and n_new & (n_new - 1) == 0
    assert n_pages % ppg == 0
    a_mat = _imp_to_block_matrix(nc, nsb_pad)
    tok = np.arange(PAGE_SIZE)[None, :, None]
    place = jnp.asarray(np.arange(PAGE_SIZE)[None, None, :] == np.arange(ppg)[:, None, None] * cpp + tok // CMP_STRIDE,
                        dtype=BF16)
    grid_spec = pltpu.PrefetchScalarGridSpec(
        num_scalar_prefetch=1,
        grid=(db,),
        in_specs=[
            pl.BlockSpec((None, rows, NSA_KV_HALF), lambda b, pt: (b, 0, 0)),
            pl.BlockSpec((None, NSA_KV_W, CMP_STRIDE), lambda b, pt: (b, 0, 0)),
            pl.BlockSpec((2, NSA_KV_W, PAGE_SIZE), lambda b, pt: (0, 0, 0)),
            pl.BlockSpec((ppg, PAGE_SIZE, PAGE_SIZE), lambda b, pt: (0, 0, 0)),
            pl.BlockSpec((rows, 1), lambda b, pt: (0, 0)),
            pl.BlockSpec((nc, nsb_pad), lambda b, pt: (0, 0)),
        ] + _page_specs(n_pages, NSA_KV_W, layer),
        out_specs=[
            pl.BlockSpec((None, rows, NSA_KV_HALF), lambda b, pt: (b, 0, 0)),
            pl.BlockSpec((None, rows // NSA_GROUP, nsb_pad), lambda b, pt: (b, 0, 0)),
        ],
        scratch_shapes=[pltpu.VMEM((NSA_KV_W, nc + PAGE_SIZE), F32), pltpu.VMEM((NSA_KV_W, nc + PAGE_SIZE), F32)],
    )
    return pl.pallas_call(
        functools.partial(_nsa_cmp_sample_kernel, n_pages=n_pages, n_new=n_new, past=past, nsb=nsb),
        grid_spec=grid_spec,
        out_shape=[jax.ShapeDtypeStruct((db, rows, NSA_KV_HALF), F32),
                   jax.ShapeDtypeStruct((db, rows // NSA_GROUP, nsb_pad), F32)],
        compiler_params=_params("arbitrary"),
        name="nsa_cmp_sample",
    )(page_table, q_bd, new_rows, w12, place, slope_rows, a_mat, *([cache] * n_pages))


def _nsa_sel_sample_kernel(pt_ref, q_ref, selnew_ref, kmask_ref, news_ref, win_ref, neww_ref, slope_ref, oc_ref,
                           gate_ref, *rest, n_pages, n_new, past):
    pages = rest[:n_pages]
    o_ref = rest[n_pages]
    s_scr = rest[n_pages + 1]
    q2 = q_ref[...]
    qb = q2[:, :NSA_KV_HALF]
    rows = q2.shape[0]
    row = lax.broadcasted_iota(jnp.int32, (rows, 1), 0)
    q_pos = past + ((row // NSA_GROUP) & (n_new - 1))
    slope = slope_ref[...]
    for p in range(n_pages):
        kp = jnp.concatenate([pages[p][:NSA_KV_HALF, :].astype(BF16),
                              kmask_ref[:, p * PAGE_SIZE:(p + 1) * PAGE_SIZE]], axis=0)
        s_scr[:, p * PAGE_SIZE:(p + 1) * PAGE_SIZE] = jnp.dot(q2, kp, preferred_element_type=F32)
    k_pos = lax.broadcasted_iota(jnp.int32, (1, n_pages * PAGE_SIZE), 1)
    d = q_pos - k_pos
    s = s_scr[...] - slope * d.astype(F32)
    valid = (d >= 0) & (s > 0.1 * NEG_BIG)
    s = jnp.where(valid, s, NEG_BIG)
    news = news_ref[...]
    n_pad = news.shape[0]
    dn = q_pos - (past + lax.broadcasted_iota(jnp.int32, (1, n_pad), 1))
    valid_n = (dn >= 0) & (selnew_ref[...] > 0.5)
    s_n = (lax.dot_general(qb, news[:, :NSA_KV_HALF].astype(BF16), _NT, preferred_element_type=F32)
           - slope * dn.astype(F32))
    s_n = jnp.where(valid_n, s_n, NEG_BIG)
    m = jnp.maximum(jnp.max(s, axis=-1, keepdims=True), jnp.max(s_n, axis=-1, keepdims=True))
    e = jnp.where(valid, jnp.exp(s - m), 0.0)
    e_n = jnp.where(valid_n, jnp.exp(s_n - m), 0.0)
    l = jnp.sum(e, axis=-1, keepdims=True) + jnp.sum(e_n, axis=-1, keepdims=True)
    eb = e.astype(BF16)
    acc = jnp.dot(e_n.astype(BF16), news[:, NSA_KV_HALF:].astype(BF16), preferred_element_type=F32)
    for p in range(n_pages):
        vp = pages[p][NSA_KV_HALF:, :].astype(BF16)
        acc = acc + lax.dot_general(eb[:, p * PAGE_SIZE:(p + 1) * PAGE_SIZE], vp, _NT, preferred_element_type=F32)
    o_s = acc / jnp.maximum(l, TINY)
    kw = jnp.concatenate([win_ref[...], neww_ref[...]], axis=0)
    n_win = win_ref.shape[0]
    kw_pos = past - n_win + lax.broadcasted_iota(jnp.int32, (1, kw.shape[0]), 1)
    dw = q_pos - kw_pos
    valid_w = (dw >= 0) & (dw < WINDOW) & (kw_pos >= 0)
    s_w = (lax.dot_general(qb, kw[:, :NSA_KV_HALF].astype(BF16), _NT, preferred_element_type=F32)
           - slope * dw.astype(F32))
    e_w, l_w = _masked_softmax_parts(s_w, valid_w)
    p_w = e_w / jnp.maximum(l_w, TINY)
    o_w = jnp.dot(p_w.astype(BF16), kw[:, NSA_KV_HALF:].astype(BF16), preferred_element_type=F32)
    gate = gate_ref[...]
    o = gate[:, 0:1] * oc_ref[...] + gate[:, 1:2] * o_s + gate[:, 2:3] * o_w
    o_ref[...] = jnp.where(row < rows // NSA_KV_HEADS, o[:, :NSA_HEAD_DIM], o[:, NSA_HEAD_DIM:])


def nsa_sel_sample(page_table, q2, selnew, kmask, news, win, neww, slope_rows, o_c, gates, cache, layer, n_new):
    db, rows, _ = q2.shape
    n_pages = page_table.shape[1]
    past = n_pages * PAGE_SIZE
    n_win = win.shape[1]
    n_pad = news.shape[1]
    per_seq = lambda *shape: pl.BlockSpec((None,) + shape, lambda b, pt: (b,) + (0,) * len(shape))
    grid_spec = pltpu.PrefetchScalarGridSpec(
        num_scalar_prefetch=1,
        grid=(db,),
        in_specs=[
            per_seq(rows, q2.shape[-1]),
            per_seq(rows, 1),
            pl.BlockSpec(kmask.shape, lambda b, pt: (0, 0)),
            per_seq(n_pad, NSA_KV_W),
            per_seq(n_win, NSA_KV_W),
            per_seq(n_pad, NSA_KV_W),
            pl.BlockSpec((rows, 1), lambda b, pt: (0, 0)),
            per_seq(rows, NSA_KV_HALF),
            per_seq(rows, 3),
        ] + _page_specs(n_pages, NSA_KV_W, layer),
        out_specs=per_seq(rows, NSA_HEAD_DIM),
        scratch_shapes=[pltpu.VMEM((rows, past), F32)],
    )
    return pl.pallas_call(
        functools.partial(_nsa_sel_sample_kernel, n_pages=n_pages, n_new=n_new, past=past),
        grid_spec=grid_spec,
        out_shape=jax.ShapeDtypeStruct((db, rows, NSA_HEAD_DIM), F32),
        compiler_params=_params("arbitrary"),
        name="nsa_sel_sample",
    )(page_table, q2, selnew, kmask, news, win, neww, slope_rows, o_c, gates, *([cache] * n_pages))


def even_project(proj, pos, g_q, w_q_up, g_kv, w_kv_up):
    sizes = [MLA_Q_LORA, MLA_KV_LORA, MLA_ROPE, NSA_HEADS * NSA_HEAD_DIM, NSA_KV_W, NSA_KV_W, NSA_KV_W, 3 * NSA_HEADS]
    cuts = [int(c) for c in np.cumsum(sizes)[:-1]]
    cq, ckv, kr, nq, kvc, kvs, kvw, gt = jnp.split(proj, cuts, axis=-1)
    q = jnp.einsum('mc,chd->mhd', rmsnorm(cq, g_q), w_q_up)
    q_abs = jnp.einsum('mhn,chn->mhc', q[..., :MLA_NOPE], w_kv_up[..., :MLA_NOPE]) * MLA_SCALE
    q_rope = rope_rows(q[..., MLA_NOPE:], pos) * MLA_SCALE
    q_cat = jnp.concatenate([q_abs, q_rope], axis=-1).astype(BF16)
    ckv_rows = jnp.concatenate([rmsnorm(ckv, g_kv), rope_rows(kr, pos)], axis=-1)
    nq = (nq.reshape(-1, NSA_KV_HEADS, NSA_GROUP, NSA_HEAD_DIM) * NSA_SCALE).astype(BF16)
    gates = jax.nn.sigmoid(gt.reshape(-1, NSA_KV_HEADS, NSA_GROUP, 3))
    return q_cat, ckv_rows, nq, kvc, kvs, kvw, gates


def chunk_partials(rows, w):
    B, L = rows.shape[:2]
    rows = pad_rows(rows, L + (-L) % CMP_STRIDE)
    ch = rows.reshape((B, rows.shape[1] // CMP_STRIDE, CMP_STRIDE) + rows.shape[2:])
    first = jnp.einsum('bcjegd,ejg->bcegd', ch, w[:, :CMP_STRIDE], precision=lax.Precision.HIGHEST)
    second = jnp.einsum('bcjegd,ejg->bcegd', ch, w[:, CMP_STRIDE:], precision=lax.Precision.HIGHEST)
    return first, second


def _cmp_weight_rows(w_cmp):
    w = jnp.transpose(w_cmp, (1, 0, 2))
    w = jnp.broadcast_to(w[..., None], w.shape + (NSA_HEAD_DIM,)).reshape(CMP_BLOCK, NSA_KV_W)
    return w.reshape(2, CMP_STRIDE, NSA_KV_W)


def even_mixer_prompt(q_cat, ckv_rows, nq, kvc, kvs, kvw, gates, w_up, w_cmp, B, T):
    slopes = alibi_slopes()
    o_m = mla_prompt(jnp.transpose(q_cat.reshape(B, T, MLA_HEADS, MLA_ROW), (0, 2, 1, 3)),
                     ckv_rows.reshape(B, T, MLA_ROW).astype(BF16), w_up)
    q = jnp.transpose(nq.reshape(B, T, NSA_KV_HEADS, NSA_GROUP, NSA_HEAD_DIM), (0, 2, 3, 1, 4))
    kv5 = (B, T, 2, NSA_KV_HEADS, NSA_HEAD_DIM)
    first, second = chunk_partials(kvc.reshape(kv5), w_cmp)
    kc = first[:, :-1] + second[:, 1:]
    n_cmp = kc.shape[1]
    n_pad = -(-n_cmp // 128) * 128
    kc = jnp.transpose(pad_rows(kc, n_pad), (2, 0, 3, 1, 4)).astype(BF16)
    o_c, sel = nsa_cmp_prompt(q, kc[0], kc[1], slopes, n_cmp)
    nsb = sel.shape[-1]
    q_sel = jnp.concatenate([q, jnp.broadcast_to((1.0 - sel).astype(BF16)[:, :, None], q.shape[:4] + (nsb,))], axis=-1)
    ks = jnp.transpose(kvs.reshape(kv5), (2, 0, 3, 1, 4)).astype(BF16)
    own_blk = (jnp.arange(T)[:, None] // SEL_BLOCK == jnp.arange(nsb)[None, :])
    k_mask = jnp.where(own_blk, MASK_FILL, 0.0).astype(BF16)
    k_sel = jnp.concatenate([ks[0], jnp.broadcast_to(k_mask, ks[0].shape[:3] + (nsb,))], axis=-1)
    o_s = nsa_flash_prompt(q_sel, k_sel, ks[1], slopes, n_band=0)
    kw = jnp.transpose(kvw.reshape(kv5), (2, 0, 3, 1, 4)).astype(BF16)
    o_w = nsa_flash_prompt(q, kw[0], kw[1], slopes, n_band=WINDOW // ATT_TILE + 1)
    gt = gates.reshape(B, T, NSA_HEADS, 3)

    def gated(o, k):
        return o.reshape(B, T, NSA_HEADS, NSA_HEAD_DIM) * gt[..., k:k + 1]

    o_n = (gated(o_c, 0) + gated(o_s, 1) + gated(o_w, 2)).reshape(B, T, NSA_HEADS * NSA_HEAD_DIM)
    return jnp.concatenate([o_m, o_n], axis=-1)


def even_mixer_sample(q_cat, ckv_rows, nq, kvc, kvs, kvw, gates, w_up, w_cmp, page_table,
                      c_mla, c_cmp, c_sel, c_win, layer, DB, DS):
    n_pages = page_table.shape[1]
    past = n_pages * PAGE_SIZE
    slopes = alibi_slopes()
    n_pad = CMP_STRIDE

    def pad_new(a):
        return pad_rows(a.reshape(DB, DS, -1), n_pad)

    q_m = jnp.transpose(q_cat.reshape(DB, DS, MLA_HEADS, MLA_ROW), (0, 2, 1, 3)).reshape(DB, MLA_HEADS * DS, MLA_ROW)
    o_m = mla_sample(page_table, q_m, pad_new(ckv_rows).astype(BF16), w_up, c_mla, layer)
    rows = NSA_KV_HEADS * DS * NSA_GROUP
    q = jnp.transpose(nq.reshape(DB, DS, NSA_KV_HEADS, NSA_GROUP, NSA_HEAD_DIM), (0, 2, 1, 3, 4))
    eye = jnp.eye(NSA_KV_HEADS, dtype=BF16)
    q_bd = (q[:, :, :, :, None, :] * eye[None, :, None, None, :, None]).reshape(DB, rows, NSA_KV_HALF)
    slope_rows = jnp.broadcast_to(slopes[:, None, :], (NSA_KV_HEADS, DS, NSA_GROUP)).reshape(rows, 1)
    w12 = jnp.tile(jnp.swapaxes(_cmp_weight_rows(w_cmp), 1, 2), (1, 1, PAGE_SIZE // CMP_STRIDE))
    o_c, sel = nsa_cmp_sample(page_table, q_bd, jnp.swapaxes(pad_new(kvc), 1, 2), w12, slope_rows, c_cmp, layer, DS)
    npb = past // SEL_BLOCK
    unsel = jnp.repeat((1.0 - sel[..., :npb]).astype(BF16), NSA_GROUP, axis=1)
    q2 = jnp.concatenate([q_bd, unsel], axis=-1)
    selnew = jnp.repeat(sel[..., npb:npb + 1], NSA_GROUP, axis=1)
    own_blk = (jnp.arange(npb)[:, None] == jnp.arange(past)[None, :] // SEL_BLOCK)
    kmask = jnp.where(own_blk, NEG_BIG, 0.0).astype(BF16)
    g_rows = jnp.transpose(gates.reshape(DB, DS, NSA_KV_HEADS, NSA_GROUP, 3), (0, 2, 1, 3, 4)).reshape(DB, rows, 3)
    o_n = nsa_sel_sample(page_table, q2, selnew, kmask, pad_new(kvs), c_win.reshape(DB, -1, NSA_KV_W),
                         pad_new(kvw), slope_rows, o_c, g_rows, c_sel, layer, DS)
    o_n = jnp.transpose(o_n.reshape(DB, NSA_KV_HEADS, DS, NSA_GROUP, NSA_HEAD_DIM), (0, 2, 1, 3, 4))
    return jnp.concatenate([o_m, o_n.reshape(DB, DS, NSA_HEADS * NSA_HEAD_DIM)], axis=-1)


DN_REP = DN_V_HEADS // DN_QK_HEADS
DN_QK_W = DN_QK_HEADS * DN_K_DIM
DN_V_W = DN_V_HEADS * DN_V_DIM
_BNN = (((2,), (1,)), ((0,), (0,)))
_BNT = (((2,), (2,)), ((0,), (0,)))
_BTN = (((1,), (1,)), ((0,), (0,)))


def _bdot(a, b, dims=_BNN):
    return lax.dot_general(a, b, dims, preferred_element_type=F32)


def _dot3(a, b):
    a_hi = a.astype(BF16)
    a_lo = (a - a_hi.astype(F32)).astype(BF16)
    b_hi = b.astype(BF16)
    b_lo = (b - b_hi.astype(F32)).astype(BF16)
    return _bdot(a_hi, b_hi) + (_bdot(a_hi, b_lo) + _bdot(a_lo, b_hi))


def _softplus(x):
    return jnp.maximum(x, 0.0) + jnp.log(1.0 + jnp.exp(-jnp.abs(x)))


def _silu(x):
    return x * jax.nn.sigmoid(x)


def _l2norm(x):
    return x * lax.rsqrt(jnp.sum(x * x, axis=-1, keepdims=True) + L2_EPS)


def _short_conv(prev, x, w):
    n = x.shape[0]
    xp = jnp.concatenate([prev, x], axis=0)
    acc = xp[8:8 + n] * w[DN_CONV - 1:DN_CONV]
    for j in range(DN_CONV - 1):
        off = 8 - (DN_CONV - 1) + j
        acc = acc + xp[off:off + n] * w[j:j + 1]
    return _silu(acc)


def _pad_conv_buf(buf):
    return jnp.concatenate([jnp.zeros((8 - buf.shape[0], buf.shape[1]), F32), buf], axis=0)


def _chunk_terms(q, k, v, kk, qk, g_row, beta_row):
    c = q.shape[1]
    ii = lax.broadcasted_iota(jnp.int32, (1, c, c), 1)
    jj = lax.broadcasted_iota(jnp.int32, (1, c, c), 2)
    eye = ii == jj
    causal = ii >= jj
    g_col = jnp.sum(jnp.where(eye, g_row, 0.0), axis=2, keepdims=True)
    beta_col = jnp.sum(jnp.where(eye, beta_row, 0.0), axis=2, keepdims=True)
    gc_col = jnp.sum(jnp.where(causal, g_row, 0.0), axis=2, keepdims=True)
    gc_row = jnp.sum(jnp.where(ii <= jj, g_col, 0.0), axis=1, keepdims=True)
    decay = jnp.exp(jnp.where(causal, gc_col - gc_row, -jnp.inf))
    n_mat = jnp.where(ii > jj, -(kk * beta_col * decay), 0.0)
    t_mat = jnp.where(eye, 1.0, n_mat)
    p = n_mat
    for _ in range(int(np.log2(c)) - 1):
        p = _dot3(p, p)
        t_mat = t_mat + _dot3(t_mat, p)
    e_col = jnp.exp(gc_col)
    sol = _dot3(t_mat, jnp.concatenate([v * beta_col, k * (beta_col * e_col)], axis=-1))
    gc_last = gc_col[:, c - 1:c]
    return (sol[..., :v.shape[-1]], sol[..., v.shape[-1]:], q * e_col, k * jnp.exp(gc_last - gc_col), qk * decay,
            jnp.exp(gc_last))


def _state_step(s, u, w, qd, kd, attn, gl):
    s_b = s.astype(BF16)
    v_new = u - _bdot(w.astype(BF16), s_b)
    v_b = v_new.astype(BF16)
    o = _bdot(qd.astype(BF16), s_b) + _bdot(attn.astype(BF16), v_b)
    s = s * gl + _bdot(kd.astype(BF16), v_b, _BTN)
    return o, s


def _gated_out(o, z, g_norm):
    ms = jnp.mean(o * o, axis=-1, keepdims=True)
    return o * lax.rsqrt(ms + NORM_EPS) * g_norm * _silu(z)


def _gdn_chunk_kernel(b_ref, a_ref, alog_ref, dtb_ref, xq_ref, hq_ref, xk_ref, hk_ref, xv_ref, hv_ref,
                      wq_ref, wk_ref, wv_ref, bq_ref, bk_ref, bv_ref,
                      u_ref, w_ref, qd_ref, kd_ref, at_ref, gl_ref, *, chunk):
    i = pl.program_id(2)
    first = i == 0
    rows = xq_ref.shape[0]

    def conv(x_ref, halo_ref, buf_ref, w_ref_):
        prev = jnp.where(first, _pad_conv_buf(buf_ref[...]), halo_ref[...])
        return _short_conv(prev, x_ref[...], w_ref_[...])

    q = _l2norm(conv(xq_ref, hq_ref, bq_ref, wq_ref)) * DN_K_DIM ** -0.5
    k = _l2norm(conv(xk_ref, hk_ref, bk_ref, wk_ref))
    v = conv(xv_ref, hv_ref, bv_ref, wv_ref)
    nc = rows // chunk
    q3 = q.reshape(nc, chunk, DN_K_DIM)
    k3 = k.reshape(nc, chunk, DN_K_DIM)
    kb = k3.astype(BF16)
    kk = _bdot(kb, kb, _BNT)
    qk = _bdot(q3.astype(BF16), kb, _BNT)
    rep = lambda a: jnp.concatenate([a] * DN_REP, axis=0)
    v3 = jnp.concatenate([v[:, j * DN_V_DIM:(j + 1) * DN_V_DIM].reshape(nc, chunk, DN_V_DIM)
                          for j in range(DN_REP)], axis=0)
    g_row = jnp.concatenate([-jnp.exp(alog_ref[j]) * _softplus(a_ref[j] + dtb_ref[j]) for j in range(DN_REP)], axis=0)
    beta_row = jnp.concatenate([jax.nn.sigmoid(b_ref[j]) for j in range(DN_REP)], axis=0)
    u, w, qd, kd, attn, gl = _chunk_terms(rep(q3), rep(k3), v3, rep(kk), rep(qk), g_row, beta_row)
    for j in range(DN_REP):
        part = lambda a: a[j * nc:(j + 1) * nc].reshape(rows, a.shape[-1])
        u_ref[j] = part(u)
        w_ref[j] = part(w).astype(BF16)
        qd_ref[j] = part(qd).astype(BF16)
        kd_ref[j] = part(kd).astype(BF16)
        at_ref[j] = part(attn).astype(BF16)
        gl_ref[j] = jnp.broadcast_to(gl[j * nc:(j + 1) * nc], gl_ref.shape[1:])


def gdn_chunk_prompt(proj, ba_rows, conv_buf, w_conv, a_log, dt_bias, B, T, row_block=512):
    c = DN_CHUNK
    n_ch = T // c
    row_block = min(row_block, T)
    nb = T // row_block
    nc = row_block // c
    qk_blk = DN_QK_W // DN_K_DIM
    rep_w = DN_REP * DN_V_DIM

    def halo(b_, i):
        return jnp.maximum((b_ * T + i * row_block) // 8 - 1, 0)

    def x_specs(width, col):
        return [pl.BlockSpec((row_block, width), lambda b_, h, i: (b_ * nb + i, col(h))),
                pl.BlockSpec((8, width), lambda b_, h, i: (halo(b_, i), col(h)))]

    gate_spec = lambda off: pl.BlockSpec((None, DN_REP, nc, 1, c), lambda b_, h, i: (b_, off + h, i, 0, 0))
    head_spec = pl.BlockSpec((DN_REP, 1, 1), lambda b_, h, i: (h, 0, 0))
    per_head = lambda width, dt: jax.ShapeDtypeStruct((B, DN_V_HEADS, T, width), dt)
    out_spec = lambda width: pl.BlockSpec((None, DN_REP, row_block, width), lambda b_, h, i: (b_, h, i, 0))
    v_col = lambda h: DN_CONV_DIM // rep_w - DN_V_W // rep_w + h
    return pl.pallas_call(
        functools.partial(_gdn_chunk_kernel, chunk=c),
        grid=(B, DN_QK_HEADS, nb),
        in_specs=[gate_spec(0), gate_spec(DN_V_HEADS // DN_REP), head_spec, head_spec]
        + x_specs(DN_K_DIM, lambda h: h) + x_specs(DN_K_DIM, lambda h: qk_blk + h) + x_specs(rep_w, v_col)
        + [pl.BlockSpec((DN_CONV, DN_K_DIM), lambda b_, h, i: (0, h)),
           pl.BlockSpec((DN_CONV, DN_K_DIM), lambda b_, h, i: (0, qk_blk + h)),
           pl.BlockSpec((DN_CONV, rep_w), lambda b_, h, i: (0, v_col(h))),
           pl.BlockSpec((None, DN_CONV - 1, DN_K_DIM), lambda b_, h, i: (b_, 0, h)),
           pl.BlockSpec((None, DN_CONV - 1, DN_K_DIM), lambda b_, h, i: (b_, 0, qk_blk + h)),
           pl.BlockSpec((None, DN_CONV - 1, rep_w), lambda b_, h, i: (b_, 0, v_col(h)))],
        out_specs=[out_spec(DN_V_DIM), out_spec(DN_K_DIM), out_spec(DN_K_DIM), out_spec(DN_K_DIM), out_spec(c),
                   pl.BlockSpec((None, DN_REP, nc, 1, 128), lambda b_, h, i: (b_, h, i, 0, 0))],
        out_shape=[per_head(DN_V_DIM, F32), per_head(DN_K_DIM, BF16), per_head(DN_K_DIM, BF16),
                   per_head(DN_K_DIM, BF16), per_head(c, BF16),
                   jax.ShapeDtypeStruct((B, DN_V_HEADS, n_ch, 1, 128), F32)],
        compiler_params=_params("arbitrary", "arbitrary", "arbitrary"),
        name="gdn_chunk_prompt",
    )(ba_rows, ba_rows, a_log.reshape(DN_V_HEADS, 1, 1), dt_bias.reshape(DN_V_HEADS, 1, 1),
      proj, proj, proj, proj, proj, proj, w_conv, w_conv, w_conv, conv_buf, conv_buf, conv_buf)


def _gdn_scan_kernel(s0_ref, u_ref, w_ref, qd_ref, kd_ref, at_ref, gl_ref, z_ref, gn_ref, o_ref, s_ref, *, chunk):
    n_heads = s0_ref.shape[0]
    n_ch = u_ref.shape[1] // chunk

    @pl.when(pl.program_id(2) == 0)
    def _():
        s_ref[...] = s0_ref[...]

    g_norm = gn_ref[...]
    for ci in range(n_ch):
        rows = slice(ci * chunk, (ci + 1) * chunk)
        o, s = _state_step(s_ref[...], u_ref[:, rows, :], w_ref[:, rows, :], qd_ref[:, rows, :], kd_ref[:, rows, :],
                           at_ref[:, rows, :], gl_ref[:, ci])
        s_ref[...] = s
        for j in range(n_heads):
            cols = slice(j * DN_V_DIM, (j + 1) * DN_V_DIM)
            o_ref[rows, cols] = _gated_out(o[j], z_ref[rows, cols], g_norm)


def gdn_scan_prompt(s0, terms, proj, g_norm, B, T, heads_per_step=8, row_block=512):
    u, w, qd, kd, attn, gl = terms
    hb = heads_per_step
    row_block = min(row_block, T)
    nb = T // row_block
    nc = row_block // DN_CHUNK
    z_col = DN_CONV_DIM // (hb * DN_V_DIM)
    per_head = lambda width: pl.BlockSpec((None, hb, row_block, width), lambda b_, h, i: (b_, h, i, 0))
    state_spec = pl.BlockSpec((None, hb, DN_K_DIM, DN_V_DIM), lambda b_, h, i: (b_, h, 0, 0))
    return pl.pallas_call(
        functools.partial(_gdn_scan_kernel, chunk=DN_CHUNK),
        grid=(B, DN_V_HEADS // hb, nb),
        in_specs=[state_spec, per_head(DN_V_DIM), per_head(DN_K_DIM), per_head(DN_K_DIM), per_head(DN_K_DIM),
                  per_head(DN_CHUNK), pl.BlockSpec((None, hb, nc, 1, 128), lambda b_, h, i: (b_, h, i, 0, 0)),
                  pl.BlockSpec((row_block, hb * DN_V_DIM), lambda b_, h, i: (b_ * nb + i, z_col + h)),
                  pl.BlockSpec((1, DN_V_DIM), lambda b_, h, i: (0, 0))],
        out_specs=[pl.BlockSpec((None, row_block, hb * DN_V_DIM), lambda b_, h, i: (b_, i, h)), state_spec],
        out_shape=[jax.ShapeDtypeStruct((B, T, DN_V_W), F32),
                   jax.ShapeDtypeStruct((B, DN_V_HEADS, DN_K_DIM, DN_V_DIM), F32)],
        compiler_params=_params("arbitrary", "arbitrary", "arbitrary"),
        name="gdn_scan_prompt",
    )(s0, u, w, qd, kd, attn, gl, proj, g_norm.reshape(1, DN_V_DIM))


def _gdn_sample_kernel(x_ref, buf_ref, wc_ref, bar_ref, alog_c_ref, dtb_c_ref, s0_ref, gn_ref, *rest):
    o_ref, s_ref = rest[-2:]
    x = x_ref[...]
    conv = _short_conv(_pad_conv_buf(buf_ref[...]), x[:, :DN_CONV_DIM], wc_ref[...])
    bar = bar_ref[...]
    g_row = -jnp.exp(alog_c_ref[...]) * _softplus(bar[DN_V_HEADS:] + dtb_c_ref[...])
    beta_row = jax.nn.sigmoid(bar[:DN_V_HEADS])
    g_norm = gn_ref[...]

    def heads(col0, n_heads, width, rep):
        return jnp.stack([conv[:, col0 + h * width:col0 + (h + 1) * width]
                          for h in range(n_heads) for _ in range(rep)], axis=0)

    q = _l2norm(heads(0, DN_QK_HEADS, DN_K_DIM, DN_REP)) * DN_K_DIM ** -0.5
    k = _l2norm(heads(DN_QK_W, DN_QK_HEADS, DN_K_DIM, DN_REP))
    v = heads(2 * DN_QK_W, DN_V_HEADS, DN_V_DIM, 1)
    kb = k.astype(BF16)
    u, w, qd, kd, attn, gl = _chunk_terms(q, k, v, _bdot(kb, kb, _BNT), _bdot(q.astype(BF16), kb, _BNT),
                                          g_row, beta_row)
    o, s = _state_step(s0_ref[...], u, w, qd, kd, attn, gl)
    s_ref[...] = s
    for hv in range(DN_V_HEADS):
        z = x[:, DN_CONV_DIM + hv * DN_V_DIM:DN_CONV_DIM + (hv + 1) * DN_V_DIM]
        o_ref[:, hv * DN_V_DIM:(hv + 1) * DN_V_DIM] = _gated_out(o[hv], z, g_norm)


def gdn_sample(proj, row0, ba_rows, conv_buf, s0_all, layer, s_stack, w_conv, a_log, dt_bias, g_norm, DB, DS):
    assert DS == 8 and row0 % DS == 0
    n_main = proj.shape[1]
    full = lambda shape: pl.BlockSpec(shape, lambda b: (0,) * len(shape))
    state_spec = pl.BlockSpec((None, None, DN_V_HEADS, DN_K_DIM, DN_V_DIM), lambda b: (layer, b, 0, 0, 0))
    in_specs = [pl.BlockSpec((DS, n_main), lambda b: (row0 // DS + b, 0)),
                pl.BlockSpec((None, DN_CONV - 1, DN_CONV_DIM), lambda b: (b, 0, 0)),
                full((DN_CONV, DN_CONV_DIM)),
                pl.BlockSpec((None, 2 * DN_V_HEADS, 1, DS), lambda b: (b, 0, 0, 0)),
                full((DN_V_HEADS, 1, 1)), full((DN_V_HEADS, 1, 1)),
                state_spec, full((1, DN_V_DIM))]
    args = [proj, conv_buf, w_conv, ba_rows, a_log.reshape(-1, 1, 1), dt_bias.reshape(-1, 1, 1), s0_all,
            g_norm.reshape(1, DN_V_DIM)]
    aliases = {}
    if s_stack is not None:
        in_specs.append(pl.BlockSpec(memory_space=pl.ANY))
        args.append(s_stack)
        aliases = {len(args) - 1: 1}
    return pl.pallas_call(
        _gdn_sample_kernel,
        grid=(DB,),
        in_specs=in_specs,
        out_specs=[pl.BlockSpec((None, DS, DN_V_W), lambda b: (b, 0, 0)), state_spec],
        out_shape=[jax.ShapeDtypeStruct((DB, DS, DN_V_W), F32), jax.ShapeDtypeStruct(s0_all.shape, F32)],
        input_output_aliases=aliases,
        compiler_params=_params("arbitrary"),
        name="gdn_sample",
    )(*args)


def delta_mixers(proj, proj_ba, conv_state, s0_sample, layer, s_stack, w_conv, a_log, dt_bias, g_norm, B, T, DB, DS):
    n_p = B * T
    assert T % DN_CHUNK == 0 and T >= DN_CONV - 1 and DS >= DN_CONV - 1
    n_ch = T // DN_CHUNK
    ba_p = jnp.transpose(proj_ba[:n_p].reshape(B, n_ch, DN_CHUNK, -1), (0, 3, 1, 2))[:, :, :, None, :]
    buf0 = jnp.zeros((B, DN_CONV - 1, DN_CONV_DIM), F32)
    s0 = jnp.zeros((B, DN_V_HEADS, DN_K_DIM, DN_V_DIM), F32)
    terms = gdn_chunk_prompt(proj, ba_p, buf0, w_conv, a_log, dt_bias, B, T)
    o_p, s_p = gdn_scan_prompt(s0, terms, proj, g_norm, B, T)
    c_p = jnp.stack([proj[(b + 1) * T - (DN_CONV - 1):(b + 1) * T, :DN_CONV_DIM] for b in range(B)])
    ba_s = jnp.transpose(proj_ba[n_p:].reshape(DB, DS, -1), (0, 2, 1))[:, :, None, :]
    o_s, s_stack = gdn_sample(proj, n_p, ba_s, conv_state, s0_sample, layer, s_stack, w_conv, a_log, dt_bias,
                              g_norm, DB, DS)
    x_s = proj[n_p:].reshape(DB, DS, -1)
    c_s = x_s[:, DS - (DN_CONV - 1):, :DN_CONV_DIM]
    return o_p, c_p, s_p, o_s, c_s, s_stack


def kernel(x_prompt, x_sample, cache_mla, cache_nsa_cmp, cache_nsa_sel, cache_nsa_win, state_delta,
           state_delta_conv, page_table, norm_g, w_ffn_in, w_ffn_out, w_in_even, g_q_lora, w_q_up,
           g_kv_lora, w_kv_up, w_cmp, w_o_even, w_in_odd, w_conv, a_log, dt_bias, g_dn_norm, w_out_odd):
    B, T, D = x_prompt.shape
    DB, DS, _ = x_sample.shape
    n_p = B * T
    n_s = DB * DS
    past = page_table.shape[1] * PAGE_SIZE
    x = jnp.concatenate([x_prompt.reshape(n_p, D), x_sample.reshape(n_s, D)], axis=0)
    pos = jnp.concatenate([jnp.tile(jnp.arange(T), B), jnp.tile(past + jnp.arange(DS), DB)])
    kv5 = (2, NSA_KV_HEADS, NSA_HEAD_DIM)
    c_mla = _pages_feature_major(cache_mla)
    c_cmp = _pages_feature_major(cache_nsa_cmp)
    c_sel = _pages_feature_major(cache_nsa_sel)

    mla_p, mla_s, cmp_p, cmp_s, sel_p, sel_s, win_p, win_s = [], [], [], [], [], [], [], []
    dst_p, dcv_p, dcv_s = [], [], []
    s_stack = None
    for l in range(DEPTH):
        ng = norm_g[l]
        i = l // 2
        x = ffn_half(x, ng[0], ng[1], w_ffn_in[l, 0].astype(BF16), w_ffn_out[l, 0].astype(BF16))
        if l % 2 == 0:
            n_in = w_in_even.shape[-1]
            proj = norm_proj(x, ng[2], w_in_even[i].astype(BF16), tm=256, tn=n_in)
            q_cat, ckv_rows, nq, kvc, kvs, kvw, gates = even_project(
                proj, pos, g_q_lora[i], w_q_up[i], g_kv_lora[i], w_kv_up[i])
            w_up = jnp.transpose(w_kv_up[i][..., MLA_NOPE:], (1, 0, 2)).astype(BF16)
            op = even_mixer_prompt(q_cat[:n_p], ckv_rows[:n_p], nq[:n_p], kvc[:n_p], kvs[:n_p], kvw[:n_p],
                                   gates[:n_p], w_up, w_cmp[i], B, T)
            os_ = even_mixer_sample(q_cat[n_p:], ckv_rows[n_p:], nq[n_p:], kvc[n_p:], kvs[n_p:], kvw[n_p:],
                                    gates[n_p:], w_up, w_cmp[i], page_table, c_mla, c_cmp, c_sel,
                                    cache_nsa_win[i], i, DB, DS)
            mla_p.append(ckv_rows[:n_p].reshape(B, T, MLA_ROW))
            mla_s.append(ckv_rows[n_p:].reshape(DB, DS, MLA_ROW))
            cmp_p.append(kvc[:n_p].reshape((B, T) + kv5)); cmp_s.append(kvc[n_p:].reshape((DB, DS) + kv5))
            sel_p.append(kvs[:n_p].reshape((B, T) + kv5)); sel_s.append(kvs[n_p:].reshape((DB, DS) + kv5))
            kvw_p = kvw[:n_p].reshape((B, T) + kv5)
            win_p.append(kvw_p[:, T - min(WINDOW, T):])
            win_s.append(jnp.concatenate([cache_nsa_win[i], kvw[n_p:].reshape((DB, DS) + kv5)], axis=1)[:, DS:])
            w_o = w_o_even[i]
        else:
            n_main = DN_CONV_DIM + DN_V_HEADS * DN_V_DIM
            w_in = w_in_odd[i]
            proj_main = norm_proj(x, ng[2], w_in[:, :n_main].astype(BF16), tm=512, tn=1024)
            proj_ba = norm_proj(x, ng[2], w_in[:, n_main:].astype(BF16), tm=512, tn=2 * DN_V_HEADS)
            op, c_p, S_p, os_, c_s, s_stack = delta_mixers(
                proj_main, proj_ba, state_delta_conv[i], state_delta, i, s_stack, w_conv[i], a_log[i], dt_bias[i],
                g_dn_norm[i], B, T, DB, DS)
            dst_p.append(S_p.astype(state_delta.dtype))
            dcv_p.append(c_p); dcv_s.append(c_s)
            w_o = w_out_odd[i]
        o = jnp.concatenate([op.reshape(n_p, -1), os_.reshape(n_s, -1)], axis=0)
        x = out_proj_residual(o, w_o.astype(BF16), x, ng[3])
        x = ffn_half(x, ng[4], ng[5], w_ffn_in[l, 1].astype(BF16), w_ffn_out[l, 1].astype(BF16))
    return (x[:n_p].reshape(B, T, D), x[n_p:].reshape(DB, DS, D),
            jnp.stack(mla_p), jnp.stack(mla_s),
            jnp.stack(cmp_p), jnp.stack(cmp_s),
            jnp.stack(sel_p), jnp.stack(sel_s),
            jnp.stack(win_p), jnp.stack(win_s),
            jnp.stack(dst_p), s_stack.astype(state_delta.dtype),
            jnp.stack(dcv_p), jnp.stack(dcv_s))
```

```python
import functools

import jax
import jax.numpy as jnp
import numpy as np
from jax import lax
from jax.experimental import pallas as pl
from jax.experimental.pallas import tpu as pltpu

D_MODEL = 2048
DEPTH = 4
PAGE_SIZE = 128
MLA_HEADS = 8
MLA_Q_LORA = 512
MLA_KV_LORA = 256
MLA_NOPE = 128
MLA_ROPE = 64
MLA_V = 128
MLA_ROW = MLA_KV_LORA + MLA_ROPE
ROPE_THETA = 10000.0
MLA_SCALE = (MLA_NOPE + MLA_ROPE) ** -0.5
NSA_HEADS = 16
NSA_KV_HEADS = 2
NSA_GROUP = NSA_HEADS // NSA_KV_HEADS
NSA_HEAD_DIM = 64
NSA_SCALE = NSA_HEAD_DIM ** -0.5
CMP_STRIDE = 16
CMP_BLOCK = 2 * CMP_STRIDE
SEL_BLOCK = 64
SEL_TOP = 16
SEL_RATIO = SEL_BLOCK // CMP_STRIDE
WINDOW = 512
FORCE_BONUS = 1.0e4
DN_QK_HEADS = 16
DN_V_HEADS = 32
DN_K_DIM = 128
DN_V_DIM = 128
DN_CONV = 4
DN_CHUNK = 64
DN_CONV_DIM = 2 * DN_QK_HEADS * DN_K_DIM + DN_V_HEADS * DN_V_DIM
D_FF = 5632
NORM_EPS = 1e-6
L2_EPS = 1e-6
NEG_BIG = -1e30
MASK_FILL = 2.0 * NEG_BIG
TINY = 1e-30
NSA_KV_W = 2 * NSA_KV_HEADS * NSA_HEAD_DIM
NSA_KV_HALF = NSA_KV_HEADS * NSA_HEAD_DIM

F32 = jnp.float32
BF16 = jnp.bfloat16

VMEM_LIMIT_BYTES = 56 * 1024 * 1024
ATT_TILE = 256

_NT = (((1,), (1,)), ((), ()))


def _params(*sem):
    return pltpu.CompilerParams(dimension_semantics=sem, vmem_limit_bytes=VMEM_LIMIT_BYTES)


def _ffn_kernel(x_ref, gpre_ref, gpost_ref, wg_ref, wu_ref, wo_ref, o_ref, h_scr, acc_scr):
    j = pl.program_id(1)

    @pl.when(j == 0)
    def _():
        x = x_ref[...]
        ms = jnp.mean(x * x, axis=-1, keepdims=True)
        h_scr[...] = (x * lax.rsqrt(ms + NORM_EPS) * gpre_ref[...]).astype(BF16)
        acc_scr[...] = jnp.zeros_like(acc_scr)

    h = h_scr[...]
    gate = jnp.dot(h, wg_ref[...], preferred_element_type=F32)
    up = jnp.dot(h, wu_ref[...], preferred_element_type=F32)
    a = (gate * jax.nn.sigmoid(gate) * up).astype(BF16)
    acc_scr[...] += jnp.dot(a, wo_ref[...], preferred_element_type=F32)

    @pl.when(j == pl.num_programs(1) - 1)
    def _():
        y = acc_scr[...]
        ms = jnp.mean(y * y, axis=-1, keepdims=True)
        o_ref[...] = x_ref[...] + 0.5 * (y * lax.rsqrt(ms + NORM_EPS) * gpost_ref[...])


def ffn_half(x, g_pre, g_post, w_in, w_out, tm=512, tf=512):
    m, d = x.shape
    f = w_out.shape[0]
    nf = f // tf
    assert m % tm == 0 and f % tf == 0
    return pl.pallas_call(
        _ffn_kernel,
        grid=(m // tm, nf),
        in_specs=[
            pl.BlockSpec((tm, d), lambda i, j: (i, 0)),
            pl.BlockSpec((1, d), lambda i, j: (0, 0)),
            pl.BlockSpec((1, d), lambda i, j: (0, 0)),
            pl.BlockSpec((d, tf), lambda i, j: (0, j)),
            pl.BlockSpec((d, tf), lambda i, j: (0, j + nf)),
            pl.BlockSpec((tf, d), lambda i, j: (j, 0)),
        ],
        out_specs=pl.BlockSpec((tm, d), lambda i, j: (i, 0)),
        out_shape=jax.ShapeDtypeStruct((m, d), F32),
        scratch_shapes=[pltpu.VMEM((tm, d), BF16), pltpu.VMEM((tm, d), F32)],
        compiler_params=_params("arbitrary", "arbitrary"),
        name="ffn_half",
    )(x, g_pre.reshape(1, d), g_post.reshape(1, d), w_in, w_in, w_out)


def _norm_proj_kernel(x_ref, g_ref, w_ref, o_ref, h_scr):
    @pl.when(pl.program_id(1) == 0)
    def _():
        x = x_ref[...]
        ms = jnp.mean(x * x, axis=-1, keepdims=True)
        h_scr[...] = (x * lax.rsqrt(ms + NORM_EPS) * g_ref[...]).astype(BF16)

    o_ref[...] = jnp.dot(h_scr[...], w_ref[...], preferred_element_type=F32)


def norm_proj(x, g, w, tm, tn):
    m, d = x.shape
    n = w.shape[1]
    assert m % tm == 0 and n % tn == 0
    return pl.pallas_call(
        _norm_proj_kernel,
        grid=(m // tm, n // tn),
        in_specs=[
            pl.BlockSpec((tm, d), lambda i, j: (i, 0)),
            pl.BlockSpec((1, d), lambda i, j: (0, 0)),
            pl.BlockSpec((d, tn), lambda i, j: (0, j)),
        ],
        out_specs=pl.BlockSpec((tm, tn), lambda i, j: (i, j)),
        out_shape=jax.ShapeDtypeStruct((m, n), F32),
        scratch_shapes=[pltpu.VMEM((tm, d), BF16)],
        compiler_params=_params("arbitrary", "arbitrary"),
        name="norm_proj",
    )(x, g.reshape(1, d), w)


def _out_proj_kernel(a_ref, w_ref, x_ref, g_ref, o_ref):
    y = jnp.dot(a_ref[...].astype(BF16), w_ref[...], preferred_element_type=F32)
    ms = jnp.mean(y * y, axis=-1, keepdims=True)
    o_ref[...] = x_ref[...] + y * lax.rsqrt(ms + NORM_EPS) * g_ref[...]


def out_proj_residual(a, w, x, g, tm=256):
    m, k = a.shape
    d = w.shape[1]
    assert m % tm == 0
    return pl.pallas_call(
        _out_proj_kernel,
        grid=(m // tm,),
        in_specs=[
            pl.BlockSpec((tm, k), lambda i: (i, 0)),
            pl.BlockSpec((k, d), lambda i: (0, 0)),
            pl.BlockSpec((tm, d), lambda i: (i, 0)),
            pl.BlockSpec((1, d), lambda i: (0, 0)),
        ],
        out_specs=pl.BlockSpec((tm, d), lambda i: (i, 0)),
        out_shape=jax.ShapeDtypeStruct((m, d), F32),
        compiler_params=_params("arbitrary"),
        name="out_proj_residual",
    )(a, w, x, g.reshape(1, d))


def rmsnorm(x, g):
    xf = x.astype(F32)
    y = xf * lax.rsqrt(jnp.mean(xf * xf, axis=-1, keepdims=True) + NORM_EPS)
    return (y * g.astype(F32)).astype(x.dtype)


def l2norm(x):
    xf = x.astype(F32)
    return xf * lax.rsqrt(jnp.sum(xf * xf, axis=-1, keepdims=True) + L2_EPS)


def rope_rows(x, pos):
    half = x.shape[-1] // 2
    inv = ROPE_THETA ** (-2.0 * jnp.arange(half, dtype=F32) / x.shape[-1])
    ang = pos.astype(F32)[:, None] * inv
    shape = (pos.shape[0],) + (1,) * (x.ndim - 2) + (half,)
    cos = jnp.cos(ang).reshape(shape)
    sin = jnp.sin(ang).reshape(shape)
    x1 = x[..., :half]
    x2 = x[..., half:]
    return jnp.concatenate([x1 * cos - x2 * sin, x1 * sin + x2 * cos], axis=-1)


def alibi_slopes():
    h = jnp.arange(NSA_HEADS, dtype=F32) + 1.0
    return (2.0 ** (-8.0 * h / NSA_HEADS)).reshape(NSA_KV_HEADS, NSA_GROUP)


def pad_rows(x, length):
    return jnp.pad(x, [(0, 0), (0, length - x.shape[1])] + [(0, 0)] * (x.ndim - 2))


def _masked_softmax_parts(s, valid):
    s = jnp.where(valid, s, NEG_BIG)
    m = jnp.max(s, axis=-1, keepdims=True)
    e = jnp.where(valid, jnp.exp(s - m), 0.0)
    return e, jnp.sum(e, axis=-1, keepdims=True)


def _topk_mask(score, k):
    n = score.shape[-1]
    lane = lax.broadcasted_iota(jnp.int32, score.shape, score.ndim - 1)
    sel = jnp.zeros(score.shape, F32)
    x = score
    for _ in range(k):
        m = jnp.max(x, axis=-1, keepdims=True)
        idx = jnp.min(jnp.where(x == m, lane, n), axis=-1, keepdims=True)
        hit = lane == idx
        sel = jnp.where(hit & (m > -jnp.inf), 1.0, sel)
        x = jnp.where(hit, -jnp.inf, x)
    return sel


def _topk_mask_by_rank(score, k, n_valid):
    lane = lax.broadcasted_iota(jnp.int32, score.shape, score.ndim - 1)
    rank = jnp.zeros(score.shape, F32)
    for m in range(n_valid):
        col = score[:, m:m + 1]
        beats = (col > score) | ((col == score) & (lane > m))
        rank = rank + jnp.where(beats, 1.0, 0.0)
    return jnp.where((rank < k) & (score > -jnp.inf), 1.0, 0.0)


def _block_scores(blk_imp, q_pos):
    blk = lax.broadcasted_iota(jnp.int32, blk_imp.shape, 1)
    cur = q_pos // SEL_BLOCK
    valid = blk <= cur
    forced = valid & ((blk == 0) | (blk >= cur - 1))
    return jnp.where(valid, blk_imp + FORCE_BONUS * forced.astype(F32), -jnp.inf)


def _imp_to_block_matrix(n_cmp, n_blk):
    c = np.arange(n_cmp)[:, None]
    n = np.arange(n_blk)[None, :]
    return jnp.asarray(((c // SEL_RATIO == n) | (c == SEL_RATIO * n - 1)).astype(np.float32))


def _mla_prompt_kernel(qt_ref, k_ref, vt_ref, wupt_ref, o_ref, m_scr, l_scr, acc_scr, *, tq):
    i = pl.program_id(1)
    j = pl.program_id(2)
    tk = k_ref.shape[0]
    cols = qt_ref.shape[1]
    heads = cols // tq

    @pl.when(j == 0)
    def _():
        m_scr[...] = jnp.full_like(m_scr, NEG_BIG)
        l_scr[...] = jnp.zeros_like(l_scr)
        acc_scr[...] = jnp.zeros_like(acc_scr)

    def step(diagonal):
        s = jnp.dot(k_ref[...], qt_ref[...], preferred_element_type=F32)
        if diagonal:
            col = lax.broadcasted_iota(jnp.int32, (1, cols), 1)
            t_pos = i * tq + (col & (tq - 1))
            k_pos = j * tk + lax.broadcasted_iota(jnp.int32, (tk, 1), 0)
            s = jnp.where(k_pos <= t_pos, s, MASK_FILL)
        m_old = m_scr[...]
        m_new = jnp.maximum(m_old, jnp.max(s, axis=0, keepdims=True))
        e = jnp.exp(s - m_new)
        alpha = jnp.exp(m_old - m_new)
        l_scr[...] = alpha * l_scr[...] + jnp.sum(e, axis=0, keepdims=True)
        acc_scr[...] = alpha * acc_scr[...] + jnp.dot(vt_ref[...], e.astype(BF16), preferred_element_type=F32)
        m_scr[...] = m_new

    @pl.when(j < i)
    def _():
        step(False)

    @pl.when(j == i)
    def _():
        step(True)
        o_lat = (acc_scr[...] / jnp.maximum(l_scr[...], TINY)).astype(BF16)
        for h in range(heads):
            o_ref[h * MLA_V:(h + 1) * MLA_V, :] = jnp.dot(wupt_ref[h], o_lat[:, h * tq:(h + 1) * tq],
                                                          preferred_element_type=F32)


def mla_prompt(q_cat, ckv, w_up, tile=ATT_TILE):
    b, h, t, dq = q_cat.shape
    assert t % tile == 0 and tile & (tile - 1) == 0
    nt = t // tile
    qt = jnp.transpose(q_cat.reshape(b, h, nt, tile, dq), (0, 2, 4, 1, 3)).reshape(b, nt, dq, h * tile)
    vt = jnp.swapaxes(ckv[..., :MLA_KV_LORA], 1, 2)
    o_t = pl.pallas_call(
        functools.partial(_mla_prompt_kernel, tq=tile),
        grid=(b, nt, nt),
        in_specs=[
            pl.BlockSpec((None, None, dq, h * tile), lambda b_, i, j: (b_, i, 0, 0)),
            pl.BlockSpec((None, tile, dq), lambda b_, i, j: (b_, jnp.minimum(i, j), 0)),
            pl.BlockSpec((None, MLA_KV_LORA, tile), lambda b_, i, j: (b_, 0, jnp.minimum(i, j))),
            pl.BlockSpec((h, MLA_V, MLA_KV_LORA), lambda b_, i, j: (0, 0, 0)),
        ],
        out_specs=pl.BlockSpec((None, h * MLA_V, tile), lambda b_, i, j: (b_, 0, i)),
        out_shape=jax.ShapeDtypeStruct((b, h * MLA_V, t), F32),
        scratch_shapes=[pltpu.VMEM((1, h * tile), F32), pltpu.VMEM((1, h * tile), F32),
                        pltpu.VMEM((MLA_KV_LORA, h * tile), F32)],
        compiler_params=_params("arbitrary", "arbitrary", "arbitrary"),
        name="mla_prompt",
    )(qt, ckv, vt, jnp.swapaxes(w_up, 1, 2))
    return jnp.swapaxes(o_t, 1, 2)


def _page_specs(n_pages, width, layer):
    return [pl.BlockSpec((None, None, width, PAGE_SIZE), lambda b, pt, p=p: (layer, pt[b, p], 0, 0))
            for p in range(n_pages)]


def _pages_feature_major(cache):
    c = cache.reshape(cache.shape[:3] + (-1,))
    return jnp.swapaxes(c, 2, 3)


def _mla_sample_kernel(pt_ref, q_ref, new_ref, wup_ref, *rest, n_pages, n_new):
    pages = rest[:n_pages]
    o_ref = rest[n_pages]
    s_scr = rest[n_pages + 1]
    q = q_ref[...]
    rows = q.shape[0]
    heads = rows // n_new
    for p in range(n_pages):
        kp = pages[p][...].astype(BF16)
        s_scr[:, p * PAGE_SIZE:(p + 1) * PAGE_SIZE] = jnp.dot(q, kp, preferred_element_type=F32)
    new = new_ref[...]
    s_new = lax.dot_general(q, new, _NT, preferred_element_type=F32)
    t_row = lax.broadcasted_iota(jnp.int32, (rows, 1), 0) & (n_new - 1)
    valid_new = lax.broadcasted_iota(jnp.int32, (1, new.shape[0]), 1) <= t_row
    s_new = jnp.where(valid_new, s_new, NEG_BIG)
    s = s_scr[...]
    m = jnp.maximum(jnp.max(s, axis=-1, keepdims=True), jnp.max(s_new, axis=-1, keepdims=True))
    e = jnp.exp(s - m).astype(BF16)
    e_new = jnp.where(valid_new, jnp.exp(s_new - m), 0.0)
    l = jnp.sum(e.astype(F32), axis=-1, keepdims=True) + jnp.sum(e_new, axis=-1, keepdims=True)
    acc = jnp.dot(e_new.astype(BF16), new[:, :MLA_KV_LORA], preferred_element_type=F32)
    for p in range(n_pages):
        vp = pages[p][:MLA_KV_LORA, :].astype(BF16)
        acc = acc + lax.dot_general(e[:, p * PAGE_SIZE:(p + 1) * PAGE_SIZE], vp, _NT, preferred_element_type=F32)
    o_lat = (acc / jnp.maximum(l, TINY)).astype(BF16)
    outs = [jnp.dot(o_lat[h * n_new:(h + 1) * n_new], wup_ref[h], preferred_element_type=F32) for h in range(heads)]
    o_ref[...] = jnp.concatenate(outs, axis=-1)


def mla_sample(page_table, q_cat, new_rows, w_up, cache, layer):
    db, rows, dq = q_cat.shape
    n_pages = page_table.shape[1]
    heads = w_up.shape[0]
    n_new = rows // heads
    assert n_new & (n_new - 1) == 0 and n_new <= new_rows.shape[1]
    n_pad = new_rows.shape[1]
    grid_spec = pltpu.PrefetchScalarGridSpec(
        num_scalar_prefetch=1,
        grid=(db,),
        in_specs=[
            pl.BlockSpec((None, rows, dq), lambda b, pt: (b, 0, 0)),
            pl.BlockSpec((None, n_pad, dq), lambda b, pt: (b, 0, 0)),
            pl.BlockSpec((heads, MLA_KV_LORA, MLA_V), lambda b, pt: (0, 0, 0)),
        ] + _page_specs(n_pages, dq, layer),
        out_specs=pl.BlockSpec((None, n_new, heads * MLA_V), lambda b, pt: (b, 0, 0)),
        scratch_shapes=[pltpu.VMEM((rows, n_pages * PAGE_SIZE), F32)],
    )
    return pl.pallas_call(
        functools.partial(_mla_sample_kernel, n_pages=n_pages, n_new=n_new),
        grid_spec=grid_spec,
        out_shape=jax.ShapeDtypeStruct((db, n_new, heads * MLA_V), F32),
        compiler_params=_params("arbitrary"),
        name="mla_sample",
    )(page_table, q_cat, new_rows, w_up, *([cache] * n_pages))


def _nsa_cmp_prompt_kernel(q_ref, kk_ref, kv_ref, slope_ref, a_ref, o_ref, sel_ref, *, n_cmp):
    i = pl.program_id(2)
    n_heads, tq, _ = q_ref.shape
    kk = kk_ref[...]
    kv = kv_ref[...]
    n_pad = kk.shape[0]
    t_pos = i * tq + lax.broadcasted_iota(jnp.int32, (tq, 1), 0)
    c_idx = lax.broadcasted_iota(jnp.int32, (1, n_pad), 1)
    dc = t_pos - (c_idx * CMP_STRIDE + (CMP_BLOCK - 1))
    valid = (dc >= 0) & (c_idx < n_cmp)
    dcf = dc.astype(F32)
    imp = jnp.zeros((tq, n_pad), F32)
    outs = []
    for r in range(n_heads):
        s = lax.dot_general(q_ref[r], kk, _NT, preferred_element_type=F32) - slope_ref[r] * dcf
        e, l = _masked_softmax_parts(s, valid)
        p = e / jnp.maximum(l, TINY)
        imp = imp + p
        outs.append(jnp.dot(p.astype(BF16), kv, preferred_element_type=F32))
    o_ref[...] = jnp.concatenate(outs, axis=-1)
    blk_imp = jnp.dot(imp, a_ref[...], preferred_element_type=F32, precision=lax.Precision.HIGHEST)
    sel_ref[...] = _topk_mask(_block_scores(blk_imp, t_pos), min(SEL_TOP, blk_imp.shape[-1]))


def nsa_cmp_prompt(q, kc_k, kc_v, slopes, n_cmp, tile=ATT_TILE):
    b, g, r, t, dh = q.shape
    n_pad = kc_k.shape[2]
    nsb = -(-t // SEL_BLOCK)
    a_mat = _imp_to_block_matrix(n_pad, nsb)
    return pl.pallas_call(
        functools.partial(_nsa_cmp_prompt_kernel, n_cmp=n_cmp),
        grid=(b, g, t // tile),
        in_specs=[
            pl.BlockSpec((None, None, r, tile, dh), lambda b_, g_, i: (b_, g_, 0, i, 0)),
            pl.BlockSpec((None, None, n_pad, dh), lambda b_, g_, i: (b_, g_, 0, 0)),
            pl.BlockSpec((None, None, n_pad, dh), lambda b_, g_, i: (b_, g_, 0, 0)),
            pl.BlockSpec((None, r, 1, 1), lambda b_, g_, i: (g_, 0, 0, 0)),
            pl.BlockSpec((n_pad, nsb), lambda b_, g_, i: (0, 0)),
        ],
        out_specs=[
            pl.BlockSpec((None, tile, r * dh), lambda b_, g_, i: (b_, i, g_)),
            pl.BlockSpec((None, None, tile, nsb), lambda b_, g_, i: (b_, g_, i, 0)),
        ],
        out_shape=[jax.ShapeDtypeStruct((b, t, g * r * dh), F32), jax.ShapeDtypeStruct((b, g, t, nsb), F32)],
        compiler_params=_params("arbitrary", "arbitrary", "arbitrary"),
        name="nsa_cmp_prompt",
    )(q, kc_k, kc_v, slopes.reshape(g, r, 1, 1), a_mat)


def _nsa_flash_kernel(qt_ref, k_ref, vt_ref, slope_ref, o_ref, m_scr, l_scr, acc_scr, *, n_band, tq):
    i = pl.program_id(2)
    jj = pl.program_id(3)
    tk = k_ref.shape[0]
    n_heads = qt_ref.shape[1] // tq
    j = jj if n_band == 0 else i - (n_band - 1) + jj

    @pl.when(jj == 0)
    def _():
        m_scr[...] = jnp.full_like(m_scr, NEG_BIG)
        l_scr[...] = jnp.zeros_like(l_scr)
        acc_scr[...] = jnp.zeros_like(acc_scr)

    def step(positional_mask):
        k_pos = j * tk + lax.broadcasted_iota(jnp.int32, (tk, 1), 0)
        t_pos = i * tq + lax.broadcasted_iota(jnp.int32, (1, tq), 1)
        d = t_pos - k_pos
        d_all = jnp.concatenate([d.astype(F32)] * n_heads, axis=1)
        s = jnp.dot(k_ref[...], qt_ref[...], preferred_element_type=F32) - slope_ref[...] * d_all
        if positional_mask:
            in_range = (d_all >= 0.0) if n_band == 0 else ((d_all >= 0.0) & (d_all < float(WINDOW)))
            s = jnp.where(in_range, s, MASK_FILL)
        m_old = m_scr[...]
        m_new = jnp.maximum(m_old, jnp.max(s, axis=0, keepdims=True))
        e = jnp.exp(s - m_new)
        alpha = jnp.exp(m_old - m_new)
        l_scr[...] = alpha * l_scr[...] + jnp.sum(e, axis=0, keepdims=True)
        acc_scr[...] = alpha * acc_scr[...] + jnp.dot(vt_ref[...], e.astype(BF16), preferred_element_type=F32)
        m_scr[...] = m_new

    if n_band == 0:
        @pl.when(j < i)
        def _():
            step(False)

        @pl.when(j == i)
        def _():
            step(True)
    else:
        @pl.when(j >= 0)
        def _():
            step(True)

    @pl.when(jj == pl.num_programs(3) - 1)
    def _():
        o_ref[...] = acc_scr[...] / jnp.maximum(l_scr[...], TINY)


def nsa_flash_prompt(q, k, v, slopes, n_band, tile=ATT_TILE):
    b, g, r, t, dq = q.shape
    dh = v.shape[-1]
    nt = t // tile
    qt = jnp.transpose(q.reshape(b, g, r, nt, tile, dq), (0, 1, 3, 5, 2, 4)).reshape(b, g, nt, dq, r * tile)
    vt = jnp.swapaxes(v, 2, 3)
    slope_cols = jnp.repeat(slopes, tile, axis=1).reshape(g, 1, r * tile)
    if n_band == 0:
        n_steps = nt
        kv_idx = lambda i, jj: jnp.minimum(i, jj)
    else:
        n_steps = n_band
        kv_idx = lambda i, jj: jnp.maximum(i - (n_band - 1) + jj, 0)
    o_t = pl.pallas_call(
        functools.partial(_nsa_flash_kernel, n_band=n_band, tq=tile),
        grid=(b, g, nt, n_steps),
        in_specs=[
            pl.BlockSpec((None, None, None, dq, r * tile), lambda b_, g_, i, jj: (b_, g_, i, 0, 0)),
            pl.BlockSpec((None, None, tile, dq), lambda b_, g_, i, jj: (b_, g_, kv_idx(i, jj), 0)),
            pl.BlockSpec((None, None, dh, tile), lambda b_, g_, i, jj: (b_, g_, 0, kv_idx(i, jj))),
            pl.BlockSpec((None, 1, r * tile), lambda b_, g_, i, jj: (g_, 0, 0)),
        ],
        out_specs=pl.BlockSpec((None, None, None, dh, r * tile), lambda b_, g_, i, jj: (b_, g_, i, 0, 0)),
        out_shape=jax.ShapeDtypeStruct((b, g, nt, dh, r * tile), F32),
        scratch_shapes=[pltpu.VMEM((1, r * tile), F32), pltpu.VMEM((1, r * tile), F32),
                        pltpu.VMEM((dh, r * tile), F32)],
        compiler_params=_params("arbitrary", "arbitrary", "arbitrary", "arbitrary"),
        name="nsa_flash_sel" if n_band == 0 else "nsa_flash_win",
    )(qt, k, vt, slope_cols)
    o = jnp.transpose(o_t.reshape(b, g, nt, dh, r, tile), (0, 2, 5, 1, 4, 3))
    return o.reshape(b, t, g * r * dh)


def _dot_hi_lo(y, w01):
    hi = y.astype(BF16)
    lo = (y - hi.astype(F32)).astype(BF16)
    return jnp.dot(hi, w01, preferred_element_type=F32) + jnp.dot(lo, w01, preferred_element_type=F32)


def _nsa_cmp_sample_kernel(pt_ref, q_ref, new_ref, w_ref, place_ref, slope_ref, a_ref, *rest, n_pages, n_new, past,
                           nsb):
    pages = rest[:n_pages]
    o_ref, sel_ref = rest[n_pages:n_pages + 2]
    f_scr, s_scr = rest[n_pages + 2:]
    cpp = PAGE_SIZE // CMP_STRIDE
    ppg = PAGE_SIZE // cpp
    nc = n_pages * cpp
    w1 = w_ref[0]
    w2 = w_ref[1]
    for g0 in range(0, n_pages, ppg):
        acc_f = jnp.zeros((NSA_KV_W, PAGE_SIZE), F32)
        acc_s = jnp.zeros((NSA_KV_W, PAGE_SIZE), F32)
        for pp in range(ppg):
            x = pages[g0 + pp][...]
            place = place_ref[pp]
            acc_f = acc_f + _dot_hi_lo(x * w1, place)
            acc_s = acc_s + _dot_hi_lo(x * w2, place)
        f_scr[:, g0 * cpp:g0 * cpp + PAGE_SIZE] = acc_f
        s_scr[:, g0 * cpp:g0 * cpp + PAGE_SIZE] = acc_s
    xn = new_ref[...]
    lane = lax.broadcasted_iota(jnp.int32, (1, PAGE_SIZE), 1)
    f_scr[:, nc:nc + PAGE_SIZE] = jnp.where(lane == 0, jnp.sum(xn * w1[:, :CMP_STRIDE], axis=1, keepdims=True), 0.0)
    s_scr[:, nc:nc + PAGE_SIZE] = jnp.where(lane == 0, jnp.sum(xn * w2[:, :CMP_STRIDE], axis=1, keepdims=True), 0.0)
    s_all = s_scr[...]
    kc = f_scr[:, 0:nc] + pltpu.roll(s_all, s_all.shape[1] - 1, 1)[:, 0:nc]
    kk = kc[:NSA_KV_HALF].astype(BF16)
    kv = kc[NSA_KV_HALF:].astype(BF16)
    q = q_ref[...]
    rows = q.shape[0]
    row = lax.broadcasted_iota(jnp.int32, (rows, 1), 0)
    q_pos = past + ((row // NSA_GROUP) & (n_new - 1))
    c_idx = lax.broadcasted_iota(jnp.int32, (1, nc), 1)
    dc = q_pos - (c_idx * CMP_STRIDE + (CMP_BLOCK - 1))
    valid = dc >= 0
    s = jnp.dot(q, kk, preferred_element_type=F32) - slope_ref[...] * dc.astype(F32)
    e, l = _masked_softmax_parts(s, valid)
    p = e / jnp.maximum(l, TINY)
    o_ref[...] = lax.dot_general(p.astype(BF16), kv, _NT, preferred_element_type=F32)
    imp = jnp.sum(p.reshape(rows // NSA_GROUP, NSA_GROUP, nc), axis=1)
    blk_imp = jnp.dot(imp, a_ref[...], preferred_element_type=F32, precision=lax.Precision.HIGHEST)
    row_g = lax.broadcasted_iota(jnp.int32, (rows // NSA_GROUP, 1), 0)
    score = _block_scores(blk_imp, past + (row_g & (n_new - 1)))
    blk = lax.broadcasted_iota(jnp.int32, score.shape, 1)
    score = jnp.where(blk < nsb, score, -jnp.inf)
    sel_ref[...] = _topk_mask_by_rank(score, min(SEL_TOP, nsb), nsb)


def nsa_cmp_sample(page_table, q_bd, new_rows, w12, slope_rows, cache, layer, n_new):
    db, rows, _ = q_bd.shape
    n_pages = page_table.shape[1]
    past = n_pages * PAGE_SIZE
    cpp = PAGE_SIZE // CMP_STRIDE
    ppg = PAGE_SIZE // cpp
    nc = past // CMP_STRIDE
    nsb = -(-(past + n_new) // SEL_BLOCK)
    nsb_pad = -(-nsb // 128) * 128
    assert n_new <= CMP_STRIDE and new_rows.shape[2] == CMP_STRIDE and n_new & (n_new - 1) == 0
    assert n_pages % ppg == 0
    a_mat = _imp_to_block_matrix(nc, nsb_pad)
    tok = np.arange(PAGE_SIZE)[None, :, None]
    place = jnp.asarray(np.arange(PAGE_SIZE)[None, None, :] == np.arange(ppg)[:, None, None] * cpp + tok // CMP_STRIDE,
                        dtype=BF16)
    grid_spec = pltpu.PrefetchScalarGridSpec(
        num_scalar_prefetch=1,
        grid=(db,),
        in_specs=[
            pl.BlockSpec((None, rows, NSA_KV_HALF), lambda b, pt: (b, 0, 0)),
            pl.BlockSpec((None, NSA_KV_W, CMP_STRIDE), lambda b, pt: (b, 0, 0)),
            pl.BlockSpec((2, NSA_KV_W, PAGE_SIZE), lambda b, pt: (0, 0, 0)),
            pl.BlockSpec((ppg, PAGE_SIZE, PAGE_SIZE), lambda b, pt: (0, 0, 0)),
            pl.BlockSpec((rows, 1), lambda b, pt: (0, 0)),
            pl.BlockSpec((nc, nsb_pad), lambda b, pt: (0, 0)),
        ] + _page_specs(n_pages, NSA_KV_W, layer),
        out_specs=[
            pl.BlockSpec((None, rows, NSA_KV_HALF), lambda b, pt: (b, 0, 0)),
            pl.BlockSpec((None, rows // NSA_GROUP, nsb_pad), lambda b, pt: (b, 0, 0)),
        ],
        scratch_shapes=[pltpu.VMEM((NSA_KV_W, nc + PAGE_SIZE), F32), pltpu.VMEM((NSA_KV_W, nc + PAGE_SIZE), F32)],
    )
    return pl.pallas_call(
        functools.partial(_nsa_cmp_sample_kernel, n_pages=n_pages, n_new=n_new, past=past, nsb=nsb),
        grid_spec=grid_spec,
        out_shape=[jax.ShapeDtypeStruct((db, rows, NSA_KV_HALF), F32),
                   jax.ShapeDtypeStruct((db, rows // NSA_GROUP, nsb_pad), F32)],
        compiler_params=_params("arbitrary"),
        name="nsa_cmp_sample",
    )(page_table, q_bd, new_rows, w12, place, slope_rows, a_mat, *([cache] * n_pages))


def _nsa_sel_sample_kernel(pt_ref, q_ref, selnew_ref, kmask_ref, news_ref, win_ref, neww_ref, slope_ref, oc_ref,
                           gate_ref, *rest, n_pages, n_new, past):
    pages = rest[:n_pages]
    o_ref = rest[n_pages]
    s_scr = rest[n_pages + 1]
    q2 = q_ref[...]
    qb = q2[:, :NSA_KV_HALF]
    rows = q2.shape[0]
    row = lax.broadcasted_iota(jnp.int32, (rows, 1), 0)
    q_pos = past + ((row // NSA_GROUP) & (n_new - 1))
    slope = slope_ref[...]
    for p in range(n_pages):
        kp = jnp.concatenate([pages[p][:NSA_KV_HALF, :].astype(BF16),
                              kmask_ref[:, p * PAGE_SIZE:(p + 1) * PAGE_SIZE]], axis=0)
        s_scr[:, p * PAGE_SIZE:(p + 1) * PAGE_SIZE] = jnp.dot(q2, kp, preferred_element_type=F32)
    k_pos = lax.broadcasted_iota(jnp.int32, (1, n_pages * PAGE_SIZE), 1)
    d = q_pos - k_pos
    s = s_scr[...] - slope * d.astype(F32)
    news = news_ref[...]
    n_pad = news.shape[0]
    dn = q_pos - (past + lax.broadcasted_iota(jnp.int32, (1, n_pad), 1))
    valid_n = (dn >= 0) & (selnew_ref[...] > 0.5)
    s_n = (lax.dot_general(qb, news[:, :NSA_KV_HALF].astype(BF16), _NT, preferred_element_type=F32)
           - slope * dn.astype(F32))
    s_n = jnp.where(valid_n, s_n, MASK_FILL)
    m = jnp.maximum(jnp.maximum(jnp.max(s, axis=-1, keepdims=True), jnp.max(s_n, axis=-1, keepdims=True)), NEG_BIG)
    e = jnp.exp(s - m)
    e_n = jnp.exp(s_n - m)
    l = jnp.sum(e, axis=-1, keepdims=True) + jnp.sum(e_n, axis=-1, keepdims=True)
    eb = e.astype(BF16)
    acc = jnp.dot(e_n.astype(BF16), news[:, NSA_KV_HALF:].astype(BF16), preferred_element_type=F32)
    for p in range(n_pages):
        vp = pages[p][NSA_KV_HALF:, :].astype(BF16)
        acc = acc + lax.dot_general(eb[:, p * PAGE_SIZE:(p + 1) * PAGE_SIZE], vp, _NT, preferred_element_type=F32)
    o_s = acc / jnp.maximum(l, TINY)
    kw = jnp.concatenate([win_ref[...], neww_ref[...]], axis=0)
    n_win = win_ref.shape[0]
    kw_pos = past - n_win + lax.broadcasted_iota(jnp.int32, (1, kw.shape[0]), 1)
    dw = q_pos - kw_pos
    valid_w = (dw >= 0) & (dw < WINDOW) & (kw_pos >= 0)
    s_w = (lax.dot_general(qb, kw[:, :NSA_KV_HALF].astype(BF16), _NT, preferred_element_type=F32)
           - slope * dw.astype(F32))
    e_w, l_w = _masked_softmax_parts(s_w, valid_w)
    p_w = e_w / jnp.maximum(l_w, TINY)
    o_w = jnp.dot(p_w.astype(BF16), kw[:, NSA_KV_HALF:].astype(BF16), preferred_element_type=F32)
    gate = gate_ref[...]
    o = gate[:, 0:1] * oc_ref[...] + gate[:, 1:2] * o_s + gate[:, 2:3] * o_w
    o_ref[...] = jnp.where(row < rows // NSA_KV_HEADS, o[:, :NSA_HEAD_DIM], o[:, NSA_HEAD_DIM:])


def nsa_sel_sample(page_table, q2, selnew, kmask, news, win, neww, slope_rows, o_c, gates, cache, layer, n_new):
    db, rows, _ = q2.shape
    n_pages = page_table.shape[1]
    past = n_pages * PAGE_SIZE
    n_win = win.shape[1]
    n_pad = news.shape[1]
    per_seq = lambda *shape: pl.BlockSpec((None,) + shape, lambda b, pt: (b,) + (0,) * len(shape))
    grid_spec = pltpu.PrefetchScalarGridSpec(
        num_scalar_prefetch=1,
        grid=(db,),
        in_specs=[
            per_seq(rows, q2.shape[-1]),
            per_seq(rows, 1),
            pl.BlockSpec(kmask.shape, lambda b, pt: (0, 0)),
            per_seq(n_pad, NSA_KV_W),
            per_seq(n_win, NSA_KV_W),
            per_seq(n_pad, NSA_KV_W),
            pl.BlockSpec((rows, 1), lambda b, pt: (0, 0)),
            per_seq(rows, NSA_KV_HALF),
            per_seq(rows, 3),
        ] + _page_specs(n_pages, NSA_KV_W, layer),
        out_specs=per_seq(rows, NSA_HEAD_DIM),
        scratch_shapes=[pltpu.VMEM((rows, past), F32)],
    )
    return pl.pallas_call(
        functools.partial(_nsa_sel_sample_kernel, n_pages=n_pages, n_new=n_new, past=past),
        grid_spec=grid_spec,
        out_shape=jax.ShapeDtypeStruct((db, rows, NSA_HEAD_DIM), F32),
        compiler_params=_params("arbitrary"),
        name="nsa_sel_sample",
    )(page_table, q2, selnew, kmask, news, win, neww, slope_rows, o_c, gates, *([cache] * n_pages))


def even_project(proj, pos, g_q, w_q_up, g_kv, w_kv_up):
    sizes = [MLA_Q_LORA, MLA_KV_LORA, MLA_ROPE, NSA_HEADS * NSA_HEAD_DIM, NSA_KV_W, NSA_KV_W, NSA_KV_W, 3 * NSA_HEADS]
    cuts = [int(c) for c in np.cumsum(sizes)[:-1]]
    cq, ckv, kr, nq, kvc, kvs, kvw, gt = jnp.split(proj, cuts, axis=-1)
    q = jnp.einsum('mc,chd->mhd', rmsnorm(cq, g_q), w_q_up)
    q_abs = jnp.einsum('mhn,chn->mhc', q[..., :MLA_NOPE], w_kv_up[..., :MLA_NOPE]) * MLA_SCALE
    q_rope = rope_rows(q[..., MLA_NOPE:], pos) * MLA_SCALE
    q_cat = jnp.concatenate([q_abs, q_rope], axis=-1).astype(BF16)
    ckv_rows = jnp.concatenate([rmsnorm(ckv, g_kv), rope_rows(kr, pos)], axis=-1)
    nq = (nq.reshape(-1, NSA_KV_HEADS, NSA_GROUP, NSA_HEAD_DIM) * NSA_SCALE).astype(BF16)
    gates = jax.nn.sigmoid(gt.reshape(-1, NSA_KV_HEADS, NSA_GROUP, 3))
    return q_cat, ckv_rows, nq, kvc, kvs, kvw, gates


def chunk_partials(rows, w):
    B, L = rows.shape[:2]
    rows = pad_rows(rows, L + (-L) % CMP_STRIDE)
    ch = rows.reshape((B, rows.shape[1] // CMP_STRIDE, CMP_STRIDE) + rows.shape[2:])
    first = jnp.einsum('bcjegd,ejg->bcegd', ch, w[:, :CMP_STRIDE], precision=lax.Precision.HIGHEST)
    second = jnp.einsum('bcjegd,ejg->bcegd', ch, w[:, CMP_STRIDE:], precision=lax.Precision.HIGHEST)
    return first, second


def _cmp_weight_rows(w_cmp):
    w = jnp.transpose(w_cmp, (1, 0, 2))
    w = jnp.broadcast_to(w[..., None], w.shape + (NSA_HEAD_DIM,)).reshape(CMP_BLOCK, NSA_KV_W)
    return w.reshape(2, CMP_STRIDE, NSA_KV_W)


def even_mixer_prompt(q_cat, ckv_rows, nq, kvc, kvs, kvw, gates, w_up, w_cmp, B, T):
    slopes = alibi_slopes()
    o_m = mla_prompt(jnp.transpose(q_cat.reshape(B, T, MLA_HEADS, MLA_ROW), (0, 2, 1, 3)),
                     ckv_rows.reshape(B, T, MLA_ROW).astype(BF16), w_up)
    q = jnp.transpose(nq.reshape(B, T, NSA_KV_HEADS, NSA_GROUP, NSA_HEAD_DIM), (0, 2, 3, 1, 4))
    kv5 = (B, T, 2, NSA_KV_HEADS, NSA_HEAD_DIM)
    first, second = chunk_partials(kvc.reshape(kv5), w_cmp)
    kc = first[:, :-1] + second[:, 1:]
    n_cmp = kc.shape[1]
    n_pad = -(-n_cmp // 128) * 128
    kc = jnp.transpose(pad_rows(kc, n_pad), (2, 0, 3, 1, 4)).astype(BF16)
    o_c, sel = nsa_cmp_prompt(q, kc[0], kc[1], slopes, n_cmp)
    nsb = sel.shape[-1]
    q_sel = jnp.concatenate([q, jnp.broadcast_to((1.0 - sel).astype(BF16)[:, :, None], q.shape[:4] + (nsb,))], axis=-1)
    ks = jnp.transpose(kvs.reshape(kv5), (2, 0, 3, 1, 4)).astype(BF16)
    own_blk = (jnp.arange(T)[:, None] // SEL_BLOCK == jnp.arange(nsb)[None, :])
    k_mask = jnp.where(own_blk, MASK_FILL, 0.0).astype(BF16)
    k_sel = jnp.concatenate([ks[0], jnp.broadcast_to(k_mask, ks[0].shape[:3] + (nsb,))], axis=-1)
    o_s = nsa_flash_prompt(q_sel, k_sel, ks[1], slopes, n_band=0)
    kw = jnp.transpose(kvw.reshape(kv5), (2, 0, 3, 1, 4)).astype(BF16)
    o_w = nsa_flash_prompt(q, kw[0], kw[1], slopes, n_band=WINDOW // ATT_TILE + 1)
    gt = gates.reshape(B, T, NSA_HEADS, 3)

    def gated(o, k):
        return o.reshape(B, T, NSA_HEADS, NSA_HEAD_DIM) * gt[..., k:k + 1]

    o_n = (gated(o_c, 0) + gated(o_s, 1) + gated(o_w, 2)).reshape(B, T, NSA_HEADS * NSA_HEAD_DIM)
    return jnp.concatenate([o_m, o_n], axis=-1)


def even_mixer_sample(q_cat, ckv_rows, nq, kvc, kvs, kvw, gates, w_up, w_cmp, page_table,
                      c_mla, c_cmp, c_sel, c_win, layer, DB, DS):
    n_pages = page_table.shape[1]
    past = n_pages * PAGE_SIZE
    slopes = alibi_slopes()
    n_pad = CMP_STRIDE

    def pad_new(a):
        return pad_rows(a.reshape(DB, DS, -1), n_pad)

    q_m = jnp.transpose(q_cat.reshape(DB, DS, MLA_HEADS, MLA_ROW), (0, 2, 1, 3)).reshape(DB, MLA_HEADS * DS, MLA_ROW)
    o_m = mla_sample(page_table, q_m, pad_new(ckv_rows).astype(BF16), w_up, c_mla, layer)
    rows = NSA_KV_HEADS * DS * NSA_GROUP
    q = jnp.transpose(nq.reshape(DB, DS, NSA_KV_HEADS, NSA_GROUP, NSA_HEAD_DIM), (0, 2, 1, 3, 4))
    eye = jnp.eye(NSA_KV_HEADS, dtype=BF16)
    q_bd = (q[:, :, :, :, None, :] * eye[None, :, None, None, :, None]).reshape(DB, rows, NSA_KV_HALF)
    slope_rows = jnp.broadcast_to(slopes[:, None, :], (NSA_KV_HEADS, DS, NSA_GROUP)).reshape(rows, 1)
    w12 = jnp.tile(jnp.swapaxes(_cmp_weight_rows(w_cmp), 1, 2), (1, 1, PAGE_SIZE // CMP_STRIDE))
    o_c, sel = nsa_cmp_sample(page_table, q_bd, jnp.swapaxes(pad_new(kvc), 1, 2), w12, slope_rows, c_cmp, layer, DS)
    npb = past // SEL_BLOCK
    unsel = jnp.repeat((1.0 - sel[..., :npb]).astype(BF16), NSA_GROUP, axis=1)
    q2 = jnp.concatenate([q_bd, unsel], axis=-1)
    selnew = jnp.repeat(sel[..., npb:npb + 1], NSA_GROUP, axis=1)
    own_blk = (jnp.arange(npb)[:, None] == jnp.arange(past)[None, :] // SEL_BLOCK)
    kmask = jnp.where(own_blk, MASK_FILL, 0.0).astype(BF16)
    g_rows = jnp.transpose(gates.reshape(DB, DS, NSA_KV_HEADS, NSA_GROUP, 3), (0, 2, 1, 3, 4)).reshape(DB, rows, 3)
    o_n = nsa_sel_sample(page_table, q2, selnew, kmask, pad_new(kvs), c_win.reshape(DB, -1, NSA_KV_W),
                         pad_new(kvw), slope_rows, o_c, g_rows, c_sel, layer, DS)
    o_n = jnp.transpose(o_n.reshape(DB, NSA_KV_HEADS, DS, NSA_GROUP, NSA_HEAD_DIM), (0, 2, 1, 3, 4))
    return jnp.concatenate([o_m, o_n.reshape(DB, DS, NSA_HEADS * NSA_HEAD_DIM)], axis=-1)


DN_REP = DN_V_HEADS // DN_QK_HEADS
DN_QK_W = DN_QK_HEADS * DN_K_DIM
DN_V_W = DN_V_HEADS * DN_V_DIM
_BNN = (((2,), (1,)), ((0,), (0,)))
_BNT = (((2,), (2,)), ((0,), (0,)))
_BTN = (((1,), (1,)), ((0,), (0,)))


def _bdot(a, b, dims=_BNN):
    return lax.dot_general(a, b, dims, preferred_element_type=F32)


def _dot3(a, b):
    a_hi = a.astype(BF16)
    a_lo = (a - a_hi.astype(F32)).astype(BF16)
    b_hi = b.astype(BF16)
    b_lo = (b - b_hi.astype(F32)).astype(BF16)
    return _bdot(a_hi, b_hi) + (_bdot(a_hi, b_lo) + _bdot(a_lo, b_hi))


def _softplus(x):
    return jnp.maximum(x, 0.0) + jnp.log(1.0 + jnp.exp(-jnp.abs(x)))


def _silu(x):
    return x * jax.nn.sigmoid(x)


def _l2norm(x):
    return x * lax.rsqrt(jnp.sum(x * x, axis=-1, keepdims=True) + L2_EPS)


def _short_conv(prev, x, w):
    n = x.shape[0]
    xp = jnp.concatenate([prev, x], axis=0)
    acc = xp[8:8 + n] * w[DN_CONV - 1:DN_CONV]
    for j in range(DN_CONV - 1):
        off = 8 - (DN_CONV - 1) + j
        acc = acc + xp[off:off + n] * w[j:j + 1]
    return _silu(acc)


def _pad_conv_buf(buf):
    return jnp.concatenate([jnp.zeros((8 - buf.shape[0], buf.shape[1]), F32), buf], axis=0)


def _chunk_terms(q, k, v, kk, qk, g_row, beta_row):
    c = q.shape[1]
    ii = lax.broadcasted_iota(jnp.int32, (1, c, c), 1)
    jj = lax.broadcasted_iota(jnp.int32, (1, c, c), 2)
    eye = ii == jj
    causal = ii >= jj
    g_col = jnp.sum(jnp.where(eye, g_row, 0.0), axis=2, keepdims=True)
    beta_col = jnp.sum(jnp.where(eye, beta_row, 0.0), axis=2, keepdims=True)
    gc_col = jnp.sum(jnp.where(causal, g_row, 0.0), axis=2, keepdims=True)
    gc_row = jnp.sum(jnp.where(ii <= jj, g_col, 0.0), axis=1, keepdims=True)
    decay = jnp.exp(jnp.where(causal, gc_col - gc_row, -jnp.inf))
    n_mat = jnp.where(ii > jj, -(kk * beta_col * decay), 0.0)
    t_mat = jnp.where(eye, 1.0, n_mat)
    p = n_mat
    for _ in range(int(np.log2(c)) - 1):
        p = _dot3(p, p)
        t_mat = t_mat + _dot3(t_mat, p)
    e_col = jnp.exp(gc_col)
    sol = _dot3(t_mat, jnp.concatenate([v * beta_col, k * (beta_col * e_col)], axis=-1))
    gc_last = gc_col[:, c - 1:c]
    return (sol[..., :v.shape[-1]], sol[..., v.shape[-1]:], q * e_col, k * jnp.exp(gc_last - gc_col), qk * decay,
            jnp.exp(gc_last))


def _state_step(s, u, w, qd, kd, attn, gl):
    s_b = s.astype(BF16)
    v_new = u - _bdot(w.astype(BF16), s_b)
    v_b = v_new.astype(BF16)
    o = _bdot(qd.astype(BF16), s_b) + _bdot(attn.astype(BF16), v_b)
    s = s * gl + _bdot(kd.astype(BF16), v_b, _BTN)
    return o, s


def _gated_out(o, z, g_norm):
    ms = jnp.mean(o * o, axis=-1, keepdims=True)
    return o * lax.rsqrt(ms + NORM_EPS) * g_norm * _silu(z)


def _gdn_chunk_kernel(b_ref, a_ref, alog_ref, dtb_ref, xq_ref, hq_ref, xk_ref, hk_ref, xv_ref, hv_ref,
                      wq_ref, wk_ref, wv_ref, bq_ref, bk_ref, bv_ref,
                      u_ref, w_ref, qd_ref, kd_ref, at_ref, gl_ref, *, chunk):
    i = pl.program_id(2)
    first = i == 0
    rows = xq_ref.shape[0]

    def conv(x_ref, halo_ref, buf_ref, w_ref_):
        prev = jnp.where(first, _pad_conv_buf(buf_ref[...]), halo_ref[...])
        return _short_conv(prev, x_ref[...], w_ref_[...])

    q = _l2norm(conv(xq_ref, hq_ref, bq_ref, wq_ref)) * DN_K_DIM ** -0.5
    k = _l2norm(conv(xk_ref, hk_ref, bk_ref, wk_ref))
    v = conv(xv_ref, hv_ref, bv_ref, wv_ref)
    nc = rows // chunk
    q3 = q.reshape(nc, chunk, DN_K_DIM)
    k3 = k.reshape(nc, chunk, DN_K_DIM)
    kb = k3.astype(BF16)
    kk = _bdot(kb, kb, _BNT)
    qk = _bdot(q3.astype(BF16), kb, _BNT)
    rep = lambda a: jnp.concatenate([a] * DN_REP, axis=0)
    v3 = jnp.concatenate([v[:, j * DN_V_DIM:(j + 1) * DN_V_DIM].reshape(nc, chunk, DN_V_DIM)
                          for j in range(DN_REP)], axis=0)
    g_row = jnp.concatenate([-jnp.exp(alog_ref[j]) * _softplus(a_ref[j] + dtb_ref[j]) for j in range(DN_REP)], axis=0)
    beta_row = jnp.concatenate([jax.nn.sigmoid(b_ref[j]) for j in range(DN_REP)], axis=0)
    u, w, qd, kd, attn, gl = _chunk_terms(rep(q3), rep(k3), v3, rep(kk), rep(qk), g_row, beta_row)
    for j in range(DN_REP):
        part = lambda a: a[j * nc:(j + 1) * nc].reshape(rows, a.shape[-1])
        u_ref[j] = part(u)
        w_ref[j] = part(w).astype(BF16)
        qd_ref[j] = part(qd).astype(BF16)
        kd_ref[j] = part(kd).astype(BF16)
        at_ref[j] = part(attn).astype(BF16)
        gl_ref[j] = jnp.broadcast_to(gl[j * nc:(j + 1) * nc], gl_ref.shape[1:])


def gdn_chunk_prompt(proj, ba_rows, conv_buf, w_conv, a_log, dt_bias, B, T, row_block=512):
    c = DN_CHUNK
    n_ch = T // c
    row_block = min(row_block, T)
    nb = T // row_block
    nc = row_block // c
    qk_blk = DN_QK_W // DN_K_DIM
    rep_w = DN_REP * DN_V_DIM

    def halo(b_, i):
        return jnp.maximum((b_ * T + i * row_block) // 8 - 1, 0)

    def x_specs(width, col):
        return [pl.BlockSpec((row_block, width), lambda b_, h, i: (b_ * nb + i, col(h))),
                pl.BlockSpec((8, width), lambda b_, h, i: (halo(b_, i), col(h)))]

    gate_spec = lambda off: pl.BlockSpec((None, DN_REP, nc, 1, c), lambda b_, h, i: (b_, off + h, i, 0, 0))
    head_spec = pl.BlockSpec((DN_REP, 1, 1), lambda b_, h, i: (h, 0, 0))
    per_head = lambda width, dt: jax.ShapeDtypeStruct((B, DN_V_HEADS, T, width), dt)
    out_spec = lambda width: pl.BlockSpec((None, DN_REP, row_block, width), lambda b_, h, i: (b_, h, i, 0))
    v_col = lambda h: DN_CONV_DIM // rep_w - DN_V_W // rep_w + h
    return pl.pallas_call(
        functools.partial(_gdn_chunk_kernel, chunk=c),
        grid=(B, DN_QK_HEADS, nb),
        in_specs=[gate_spec(0), gate_spec(DN_V_HEADS // DN_REP), head_spec, head_spec]
        + x_specs(DN_K_DIM, lambda h: h) + x_specs(DN_K_DIM, lambda h: qk_blk + h) + x_specs(rep_w, v_col)
        + [pl.BlockSpec((DN_CONV, DN_K_DIM), lambda b_, h, i: (0, h)),
           pl.BlockSpec((DN_CONV, DN_K_DIM), lambda b_, h, i: (0, qk_blk + h)),
           pl.BlockSpec((DN_CONV, rep_w), lambda b_, h, i: (0, v_col(h))),
           pl.BlockSpec((None, DN_CONV - 1, DN_K_DIM), lambda b_, h, i: (b_, 0, h)),
           pl.BlockSpec((None, DN_CONV - 1, DN_K_DIM), lambda b_, h, i: (b_, 0, qk_blk + h)),
           pl.BlockSpec((None, DN_CONV - 1, rep_w), lambda b_, h, i: (b_, 0, v_col(h)))],
        out_specs=[out_spec(DN_V_DIM), out_spec(DN_K_DIM), out_spec(DN_K_DIM), out_spec(DN_K_DIM), out_spec(c),
                   pl.BlockSpec((None, DN_REP, nc, 1, 128), lambda b_, h, i: (b_, h, i, 0, 0))],
        out_shape=[per_head(DN_V_DIM, F32), per_head(DN_K_DIM, BF16), per_head(DN_K_DIM, BF16),
                   per_head(DN_K_DIM, BF16), per_head(c, BF16),
                   jax.ShapeDtypeStruct((B, DN_V_HEADS, n_ch, 1, 128), F32)],
        compiler_params=_params("arbitrary", "arbitrary", "arbitrary"),
        name="gdn_chunk_prompt",
    )(ba_rows, ba_rows, a_log.reshape(DN_V_HEADS, 1, 1), dt_bias.reshape(DN_V_HEADS, 1, 1),
      proj, proj, proj, proj, proj, proj, w_conv, w_conv, w_conv, conv_buf, conv_buf, conv_buf)


def _gdn_scan_kernel(s0_ref, u_ref, w_ref, qd_ref, kd_ref, at_ref, gl_ref, z_ref, gn_ref, o_ref, s_ref, *, chunk):
    n_heads = s0_ref.shape[0]
    n_ch = u_ref.shape[1] // chunk

    @pl.when(pl.program_id(2) == 0)
    def _():
        s_ref[...] = s0_ref[...]

    g_norm = gn_ref[...]
    for ci in range(n_ch):
        rows = slice(ci * chunk, (ci + 1) * chunk)
        o, s = _state_step(s_ref[...], u_ref[:, rows, :], w_ref[:, rows, :], qd_ref[:, rows, :], kd_ref[:, rows, :],
                           at_ref[:, rows, :], gl_ref[:, ci])
        s_ref[...] = s
        for j in range(n_heads):
            cols = slice(j * DN_V_DIM, (j + 1) * DN_V_DIM)
            o_ref[rows, cols] = _gated_out(o[j], z_ref[rows, cols], g_norm)


def gdn_scan_prompt(s0, terms, proj, g_norm, B, T, heads_per_step=8, row_block=512):
    u, w, qd, kd, attn, gl = terms
    hb = heads_per_step
    row_block = min(row_block, T)
    nb = T // row_block
    nc = row_block // DN_CHUNK
    z_col = DN_CONV_DIM // (hb * DN_V_DIM)
    per_head = lambda width: pl.BlockSpec((None, hb, row_block, width), lambda b_, h, i: (b_, h, i, 0))
    state_spec = pl.BlockSpec((None, hb, DN_K_DIM, DN_V_DIM), lambda b_, h, i: (b_, h, 0, 0))
    return pl.pallas_call(
        functools.partial(_gdn_scan_kernel, chunk=DN_CHUNK),
        grid=(B, DN_V_HEADS // hb, nb),
        in_specs=[state_spec, per_head(DN_V_DIM), per_head(DN_K_DIM), per_head(DN_K_DIM), per_head(DN_K_DIM),
                  per_head(DN_CHUNK), pl.BlockSpec((None, hb, nc, 1, 128), lambda b_, h, i: (b_, h, i, 0, 0)),
                  pl.BlockSpec((row_block, hb * DN_V_DIM), lambda b_, h, i: (b_ * nb + i, z_col + h)),
                  pl.BlockSpec((1, DN_V_DIM), lambda b_, h, i: (0, 0))],
        out_specs=[pl.BlockSpec((None, row_block, hb * DN_V_DIM), lambda b_, h, i: (b_, i, h)), state_spec],
        out_shape=[jax.ShapeDtypeStruct((B, T, DN_V_W), F32),
                   jax.ShapeDtypeStruct((B, DN_V_HEADS, DN_K_DIM, DN_V_DIM), F32)],
        compiler_params=_params("arbitrary", "arbitrary", "arbitrary"),
        name="gdn_scan_prompt",
    )(s0, u, w, qd, kd, attn, gl, proj, g_norm.reshape(1, DN_V_DIM))


def _gdn_sample_kernel(x_ref, buf_ref, wc_ref, bar_ref, alog_c_ref, dtb_c_ref, s0_ref, gn_ref, *rest):
    o_ref, s_ref = rest[-2:]
    x = x_ref[...]
    conv = _short_conv(_pad_conv_buf(buf_ref[...]), x[:, :DN_CONV_DIM], wc_ref[...])
    bar = bar_ref[...]
    g_row = -jnp.exp(alog_c_ref[...]) * _softplus(bar[DN_V_HEADS:] + dtb_c_ref[...])
    beta_row = jax.nn.sigmoid(bar[:DN_V_HEADS])
    g_norm = gn_ref[...]

    def heads(col0, n_heads, width, rep):
        return jnp.stack([conv[:, col0 + h * width:col0 + (h + 1) * width]
                          for h in range(n_heads) for _ in range(rep)], axis=0)

    q = _l2norm(heads(0, DN_QK_HEADS, DN_K_DIM, DN_REP)) * DN_K_DIM ** -0.5
    k = _l2norm(heads(DN_QK_W, DN_QK_HEADS, DN_K_DIM, DN_REP))
    v = heads(2 * DN_QK_W, DN_V_HEADS, DN_V_DIM, 1)
    kb = k.astype(BF16)
    u, w, qd, kd, attn, gl = _chunk_terms(q, k, v, _bdot(kb, kb, _BNT), _bdot(q.astype(BF16), kb, _BNT),
                                          g_row, beta_row)
    o, s = _state_step(s0_ref[...], u, w, qd, kd, attn, gl)
    s_ref[...] = s
    for hv in range(DN_V_HEADS):
        z = x[:, DN_CONV_DIM + hv * DN_V_DIM:DN_CONV_DIM + (hv + 1) * DN_V_DIM]
        o_ref[:, hv * DN_V_DIM:(hv + 1) * DN_V_DIM] = _gated_out(o[hv], z, g_norm)


def gdn_sample(proj, row0, ba_rows, conv_buf, s0_all, layer, s_stack, w_conv, a_log, dt_bias, g_norm, DB, DS):
    assert DS == 8 and row0 % DS == 0
    n_main = proj.shape[1]
    full = lambda shape: pl.BlockSpec(shape, lambda b: (0,) * len(shape))
    state_spec = pl.BlockSpec((None, None, DN_V_HEADS, DN_K_DIM, DN_V_DIM), lambda b: (layer, b, 0, 0, 0))
    in_specs = [pl.BlockSpec((DS, n_main), lambda b: (row0 // DS + b, 0)),
                pl.BlockSpec((None, DN_CONV - 1, DN_CONV_DIM), lambda b: (b, 0, 0)),
                full((DN_CONV, DN_CONV_DIM)),
                pl.BlockSpec((None, 2 * DN_V_HEADS, 1, DS), lambda b: (b, 0, 0, 0)),
                full((DN_V_HEADS, 1, 1)), full((DN_V_HEADS, 1, 1)),
                state_spec, full((1, DN_V_DIM))]
    args = [proj, conv_buf, w_conv, ba_rows, a_log.reshape(-1, 1, 1), dt_bias.reshape(-1, 1, 1), s0_all,
            g_norm.reshape(1, DN_V_DIM)]
    aliases = {}
    if s_stack is not None:
        in_specs.append(pl.BlockSpec(memory_space=pl.ANY))
        args.append(s_stack)
        aliases = {len(args) - 1: 1}
    return pl.pallas_call(
        _gdn_sample_kernel,
        grid=(DB,),
        in_specs=in_specs,
        out_specs=[pl.BlockSpec((None, DS, DN_V_W), lambda b: (b, 0, 0)), state_spec],
        out_shape=[jax.ShapeDtypeStruct((DB, DS, DN_V_W), F32), jax.ShapeDtypeStruct(s0_all.shape, F32)],
        input_output_aliases=aliases,
        compiler_params=_params("arbitrary"),
        name="gdn_sample",
    )(*args)


def delta_mixers(proj, proj_ba, conv_state, s0_sample, layer, s_stack, w_conv, a_log, dt_bias, g_norm, B, T, DB, DS):
    n_p = B * T
    assert T % DN_CHUNK == 0 and T >= DN_CONV - 1 and DS >= DN_CONV - 1
    n_ch = T // DN_CHUNK
    ba_p = jnp.transpose(proj_ba[:n_p].reshape(B, n_ch, DN_CHUNK, -1), (0, 3, 1, 2))[:, :, :, None, :]
    buf0 = jnp.zeros((B, DN_CONV - 1, DN_CONV_DIM), F32)
    s0 = jnp.zeros((B, DN_V_HEADS, DN_K_DIM, DN_V_DIM), F32)
    terms = gdn_chunk_prompt(proj, ba_p, buf0, w_conv, a_log, dt_bias, B, T)
    o_p, s_p = gdn_scan_prompt(s0, terms, proj, g_norm, B, T)
    c_p = jnp.stack([proj[(b + 1) * T - (DN_CONV - 1):(b + 1) * T, :DN_CONV_DIM] for b in range(B)])
    ba_s = jnp.transpose(proj_ba[n_p:].reshape(DB, DS, -1), (0, 2, 1))[:, :, None, :]
    o_s, s_stack = gdn_sample(proj, n_p, ba_s, conv_state, s0_sample, layer, s_stack, w_conv, a_log, dt_bias,
                              g_norm, DB, DS)
    x_s = proj[n_p:].reshape(DB, DS, -1)
    c_s = x_s[:, DS - (DN_CONV - 1):, :DN_CONV_DIM]
    return o_p, c_p, s_p, o_s, c_s, s_stack


def kernel(x_prompt, x_sample, cache_mla, cache_nsa_cmp, cache_nsa_sel, cache_nsa_win, state_delta,
           state_delta_conv, page_table, norm_g, w_ffn_in, w_ffn_out, w_in_even, g_q_lora, w_q_up,
           g_kv_lora, w_kv_up, w_cmp, w_o_even, w_in_odd, w_conv, a_log, dt_bias, g_dn_norm, w_out_odd):
    B, T, D = x_prompt.shape
    DB, DS, _ = x_sample.shape
    n_p = B * T
    n_s = DB * DS
    past = page_table.shape[1] * PAGE_SIZE
    x = jnp.concatenate([x_prompt.reshape(n_p, D), x_sample.reshape(n_s, D)], axis=0)
    pos = jnp.concatenate([jnp.tile(jnp.arange(T), B), jnp.tile(past + jnp.arange(DS), DB)])
    kv5 = (2, NSA_KV_HEADS, NSA_HEAD_DIM)
    c_mla = _pages_feature_major(cache_mla)
    c_cmp = _pages_feature_major(cache_nsa_cmp)
    c_sel = _pages_feature_major(cache_nsa_sel)

    mla_p, mla_s, cmp_p, cmp_s, sel_p, sel_s, win_p, win_s = [], [], [], [], [], [], [], []
    dst_p, dcv_p, dcv_s = [], [], []
    s_stack = None
    for l in range(DEPTH):
        ng = norm_g[l]
        i = l // 2
        x = ffn_half(x, ng[0], ng[1], w_ffn_in[l, 0].astype(BF16), w_ffn_out[l, 0].astype(BF16))
        if l % 2 == 0:
            n_in = w_in_even.shape[-1]
            proj = norm_proj(x, ng[2], w_in_even[i].astype(BF16), tm=256, tn=n_in)
            q_cat, ckv_rows, nq, kvc, kvs, kvw, gates = even_project(
                proj, pos, g_q_lora[i], w_q_up[i], g_kv_lora[i], w_kv_up[i])
            w_up = jnp.transpose(w_kv_up[i][..., MLA_NOPE:], (1, 0, 2)).astype(BF16)
            op = even_mixer_prompt(q_cat[:n_p], ckv_rows[:n_p], nq[:n_p], kvc[:n_p], kvs[:n_p], kvw[:n_p],
                                   gates[:n_p], w_up, w_cmp[i], B, T)
            os_ = even_mixer_sample(q_cat[n_p:], ckv_rows[n_p:], nq[n_p:], kvc[n_p:], kvs[n_p:], kvw[n_p:],
                                    gates[n_p:], w_up, w_cmp[i], page_table, c_mla, c_cmp, c_sel,
                                    cache_nsa_win[i], i, DB, DS)
            mla_p.append(ckv_rows[:n_p].reshape(B, T, MLA_ROW))
            mla_s.append(ckv_rows[n_p:].reshape(DB, DS, MLA_ROW))
            cmp_p.append(kvc[:n_p].reshape((B, T) + kv5)); cmp_s.append(kvc[n_p:].reshape((DB, DS) + kv5))
            sel_p.append(kvs[:n_p].reshape((B, T) + kv5)); sel_s.append(kvs[n_p:].reshape((DB, DS) + kv5))
            kvw_p = kvw[:n_p].reshape((B, T) + kv5)
            win_p.append(kvw_p[:, T - min(WINDOW, T):])
            win_s.append(jnp.concatenate([cache_nsa_win[i], kvw[n_p:].reshape((DB, DS) + kv5)], axis=1)[:, DS:])
            w_o = w_o_even[i]
        else:
            n_main = DN_CONV_DIM + DN_V_HEADS * DN_V_DIM
            w_in = w_in_odd[i]
            proj_main = norm_proj(x, ng[2], w_in[:, :n_main].astype(BF16), tm=512, tn=1024)
            proj_ba = norm_proj(x, ng[2], w_in[:, n_main:].astype(BF16), tm=512, tn=2 * DN_V_HEADS)
            op, c_p, S_p, os_, c_s, s_stack = delta_mixers(
                proj_main, proj_ba, state_delta_conv[i], state_delta, i, s_stack, w_conv[i], a_log[i], dt_bias[i],
                g_dn_norm[i], B, T, DB, DS)
            dst_p.append(S_p.astype(state_delta.dtype))
            dcv_p.append(c_p); dcv_s.append(c_s)
            w_o = w_out_odd[i]
        o = jnp.concatenate([op.reshape(n_p, -1), os_.reshape(n_s, -1)], axis=0)
        x = out_proj_residual(o, w_o.astype(BF16), x, ng[3])
        x = ffn_half(x, ng[4], ng[5], w_ffn_in[l, 1].astype(BF16), w_ffn_out[l, 1].astype(BF16))
    return (x[:n_p].reshape(B, T, D), x[n_p:].reshape(DB, DS, D),
            jnp.stack(mla_p), jnp.stack(mla_s),
            jnp.stack(cmp_p), jnp.stack(cmp_s),
            jnp.stack(sel_p), jnp.stack(sel_s),
            jnp.stack(win_p), jnp.stack(win_s),
            jnp.stack(dst_p), s_stack.astype(state_delta.dtype),
            jnp.stack(dcv_p), jnp.stack(dcv_s))
```
